```python
import math
import jax, jax.numpy as jnp
from jax import lax
import numpy as np

D_MODEL = 4096
BATCH = 2
SEQ = 4096
DEPTH = 1
DEC_BATCH = 128
DEC_SEQ = 1
PAST_LEN = 8192
PAGE_SIZE = 128

D_RNN = D_MODEL
LRU_BLOCKS = 16
LRU_BLOCK = D_RNN // LRU_BLOCKS
CONV_W = 4
LRU_C = 8.0
N_HEADS = 32
N_KV = 8
HEAD_DIM = 128
GROUP = N_HEADS // N_KV
D_ATTN = N_HEADS * HEAD_DIM
D_KV = N_KV * HEAD_DIM
WINDOW = 128
ROPE_THETA = 10000.0
LN_EPS = 1e-5
DN_ALPHA = (2.0 * DEPTH) ** 0.25
DN_BETA = (8.0 * DEPTH) ** -0.25
NEG_BIG = -1e30
SPLIT_POINTS = (D_RNN, 2 * D_RNN, 2 * D_RNN + D_ATTN, 2 * D_RNN + D_ATTN + D_KV,
                2 * D_RNN + D_ATTN + 2 * D_KV, 2 * D_RNN + 2 * D_ATTN + 2 * D_KV,
                2 * D_RNN + 2 * D_ATTN + 2 * D_KV + D_MODEL)
N_COLS = 2 * D_RNN + 2 * D_ATTN + 2 * D_KV + 2 * D_MODEL

kernel_name = 'hawk_swa_sink_parallel_deepnorm_step'


def _layernorm(x, g, b):
    xf = x.astype(jnp.float32)
    mu = jnp.mean(xf, axis=-1, keepdims=True)
    var = jnp.mean(jnp.square(xf - mu), axis=-1, keepdims=True)
    return ((xf - mu) * lax.rsqrt(var + LN_EPS) * g.astype(jnp.float32) + b.astype(jnp.float32)).astype(x.dtype)


def _rope(x, pos):
    half = HEAD_DIM // 2
    inv = ROPE_THETA ** (-jnp.arange(half, dtype=jnp.float32) / half)
    ang = pos.astype(jnp.float32)[:, None] * inv[None, :]
    cos = jnp.cos(ang)[None, :, None, :]
    sin = jnp.sin(ang)[None, :, None, :]
    xf = x.astype(jnp.float32)
    x1, x2 = xf[..., :half], xf[..., half:]
    return jnp.concatenate([x1 * cos - x2 * sin, x2 * cos + x1 * sin], axis=-1).astype(x.dtype)


def _project(x, w_in):
    z = x @ w_in
    return jnp.split(z, SPLIT_POINTS, axis=-1)


def _lin_combine(c1, c2):
    a1, b1 = c1
    a2, b2 = c2
    return a1 * a2, a2 * b1 + b2


def _rglru_branch(u, g_rnn, conv_buf, h0, conv_w, conv_b, w_gate_a, b_gate_a, w_gate_x, b_gate_x, lru_lambda):
    B, T, _ = u.shape
    upad = jnp.concatenate([conv_buf.astype(u.dtype), u], axis=1)
    xc = conv_b + conv_w[0] * upad[:, 0:T]
    for j in range(1, CONV_W):
        xc = xc + conv_w[j] * upad[:, j:j + T]
    new_buf = upad[:, T:]
    xb = xc.reshape(B, T, LRU_BLOCKS, LRU_BLOCK)
    r = jax.nn.sigmoid((jnp.einsum('btnd,nde->btne', xb, w_gate_a).reshape(B, T, D_RNN) + b_gate_a).astype(jnp.float32))
    i = jax.nn.sigmoid((jnp.einsum('btnd,nde->btne', xb, w_gate_x).reshape(B, T, D_RNN) + b_gate_x).astype(jnp.float32))
    log_a = -LRU_C * r * jax.nn.softplus(-lru_lambda.astype(jnp.float32))
    a = jnp.exp(log_a)
    b = jnp.sqrt(-jnp.expm1(2.0 * log_a)) * i * xc.astype(jnp.float32)
    b = b.at[:, 0].add(a[:, 0] * h0.astype(jnp.float32))
    _, h = lax.associative_scan(_lin_combine, (a, b), axis=1)
    y = h.astype(u.dtype) * jax.nn.silu(g_rnn)
    return y, new_buf, h[:, -1]


def _window_attention(q, k, v, q_pos, k_pos, sinks):
    s = jnp.einsum('bnqkgd,bnskd->bnkgqs', q.astype(jnp.float32), k.astype(jnp.float32)) * (HEAD_DIM ** -0.5)
    dq = q_pos[:, :, None] - k_pos[:, None, :]
    vis = (dq >= 0) & (dq < WINDOW) & (k_pos[:, None, :] >= 0)
    s = jnp.where(vis[None, :, None, None], s, NEG_BIG)
    sink = sinks.astype(jnp.float32).reshape(N_KV, GROUP)[None, None, :, :, None, None]
    m = jnp.maximum(jnp.max(s, axis=-1, keepdims=True), sink)
    p = jnp.exp(s - m)
    denom = jnp.sum(p, axis=-1, keepdims=True) + jnp.exp(sink - m)
    o = jnp.einsum('bnkgqs,bnskd->bnqkgd', p / denom, v.astype(jnp.float32))
    return o


def _merge_out(x, y_rnn, o_attn, g_attn, m_rnn, m_attn, w_out_rnn, w_out_attn, w_o, ln_g, ln_b):
    br_rnn = y_rnn @ w_out_rnn
    br_attn = (o_attn * jax.nn.silu(g_attn)) @ w_out_attn
    merged = jax.nn.sigmoid(m_rnn) * br_rnn + jax.nn.sigmoid(m_attn) * br_attn
    return _layernorm(DN_ALPHA * x + merged @ w_o, ln_g, ln_b)


def _prev_block(t):
    return jnp.concatenate([jnp.zeros_like(t[:, :1]), t[:, :-1]], axis=1)


def _prompt_layer(x, w_in, conv_w, conv_b, w_gate_a, b_gate_a, w_gate_x, b_gate_x, lru_lambda, sinks,
                  w_out_rnn, w_out_attn, w_o, ln_g, ln_b):
    B, S, _ = x.shape
    u, g_rnn, q, k, v, g_attn, m_rnn, m_attn = _project(x, w_in)
    conv0 = jnp.zeros((B, CONV_W - 1, D_RNN), x.dtype)
    h0 = jnp.zeros((B, D_RNN), jnp.float32)
    y_rnn, conv_new, h_new = _rglru_branch(u, g_rnn, conv0, h0, conv_w, conv_b, w_gate_a, b_gate_a,
                                           w_gate_x, b_gate_x, lru_lambda)
    pos = jnp.arange(S, dtype=jnp.int32)
    q = _rope(q.reshape(B, S, N_HEADS, HEAD_DIM), pos)
    k = _rope(k.reshape(B, S, N_KV, HEAD_DIM), pos)
    v = v.reshape(B, S, N_KV, HEAD_DIM)
    nb = S // WINDOW
    qb = q.reshape(B, nb, WINDOW, N_KV, GROUP, HEAD_DIM)
    kb = k.reshape(B, nb, WINDOW, N_KV, HEAD_DIM)
    vb = v.reshape(B, nb, WINDOW, N_KV, HEAD_DIM)
    k_band = jnp.concatenate([_prev_block(kb), kb], axis=2)
    v_band = jnp.concatenate([_prev_block(vb), vb], axis=2)
    q_pos = pos.reshape(nb, WINDOW)
    k_pos = jnp.concatenate([q_pos - WINDOW, q_pos], axis=1)
    o = _window_attention(qb, k_band, v_band, q_pos, k_pos, sinks).reshape(B, S, D_ATTN).astype(x.dtype)
    y = _merge_out(x, y_rnn, o, g_attn, m_rnn, m_attn, w_out_rnn, w_out_attn, w_o, ln_g, ln_b)
    return y, conv_new, h_new, k[:, S - WINDOW:], v[:, S - WINDOW:]


def _sample_layer(x, conv_buf, h0, k_win, v_win, w_in, conv_w, conv_b, w_gate_a, b_gate_a, w_gate_x, b_gate_x,
                  lru_lambda, sinks, w_out_rnn, w_out_attn, w_o, ln_g, ln_b):
    Bd, T, _ = x.shape
    u, g_rnn, q, k, v, g_attn, m_rnn, m_attn = _project(x, w_in)
    y_rnn, conv_new, h_new = _rglru_branch(u, g_rnn, conv_buf, h0, conv_w, conv_b, w_gate_a, b_gate_a,
                                           w_gate_x, b_gate_x, lru_lambda)
    pos = PAST_LEN + jnp.arange(T, dtype=jnp.int32)
    q = _rope(q.reshape(Bd, T, N_HEADS, HEAD_DIM), pos)
    k = _rope(k.reshape(Bd, T, N_KV, HEAD_DIM), pos)
    v = v.reshape(Bd, T, N_KV, HEAD_DIM)
    k_ctx = jnp.concatenate([k_win.astype(k.dtype), k], axis=1)
    v_ctx = jnp.concatenate([v_win.astype(v.dtype), v], axis=1)
    q_pos = pos[None]
    k_pos = jnp.concatenate([PAST_LEN - WINDOW + jnp.arange(WINDOW, dtype=jnp.int32), pos])[None]
    o = _window_attention(q.reshape(Bd, 1, T, N_KV, GROUP, HEAD_DIM), k_ctx[:, None], v_ctx[:, None],
                          q_pos, k_pos, sinks).reshape(Bd, T, D_ATTN).astype(x.dtype)
    y = _merge_out(x, y_rnn, o, g_attn, m_rnn, m_attn, w_out_rnn, w_out_attn, w_o, ln_g, ln_b)
    return y, conv_new, h_new, k_ctx[:, T:], v_ctx[:, T:]


def setup_inputs(seed: int = 0) -> dict:
    key = jax.random.key(seed)
    ks = jax.random.split(key, 20)
    f32 = jnp.float32
    u_a = jax.random.uniform(ks[10], (DEPTH, D_RNN), f32, minval=0.9, maxval=0.999)
    a0 = u_a ** (1.0 / LRU_C)
    lru_lambda = jnp.log(a0) - jnp.log1p(-a0)
    return {
        'x_prompt': jax.random.normal(ks[0], (BATCH, SEQ, D_MODEL), f32),
        'x_sample': jax.random.normal(ks[1], (DEC_BATCH, DEC_SEQ, D_MODEL), f32),
        'state_conv': jax.random.normal(ks[2], (DEPTH, DEC_BATCH, CONV_W - 1, D_RNN), f32),
        'state_lru': 0.5 * jax.random.normal(ks[3], (DEPTH, DEC_BATCH, D_RNN), f32),
        'cache_k_win': jax.random.normal(ks[4], (DEPTH, DEC_BATCH, WINDOW, N_KV, HEAD_DIM), f32),
        'cache_v_win': jax.random.normal(ks[5], (DEPTH, DEC_BATCH, WINDOW, N_KV, HEAD_DIM), f32),
        'w_in': jax.random.normal(ks[6], (DEPTH, D_MODEL, N_COLS), f32) * D_MODEL ** -0.5,
        'conv_w': jax.random.normal(ks[7], (DEPTH, CONV_W, D_RNN), f32) * CONV_W ** -0.5,
        'conv_b': 0.01 * jax.random.normal(ks[8], (DEPTH, D_RNN), f32),
        'w_gate_a': jax.random.normal(ks[9], (DEPTH, LRU_BLOCKS, LRU_BLOCK, LRU_BLOCK), f32) * LRU_BLOCK ** -0.5,
        'b_gate_a': 0.01 * jax.random.normal(ks[11], (DEPTH, D_RNN), f32),
        'w_gate_x': jax.random.normal(ks[12], (DEPTH, LRU_BLOCKS, LRU_BLOCK, LRU_BLOCK), f32) * LRU_BLOCK ** -0.5,
        'b_gate_x': 0.01 * jax.random.normal(ks[13], (DEPTH, D_RNN), f32),
        'lru_lambda': lru_lambda,
        'sinks': jax.random.normal(ks[14], (DEPTH, N_HEADS), f32),
        'w_out_rnn': jax.random.normal(ks[15], (DEPTH, D_RNN, D_MODEL), f32) * (D_RNN ** -0.5) * DN_BETA,
        'w_out_attn': jax.random.normal(ks[16], (DEPTH, D_ATTN, D_MODEL), f32) * (D_ATTN ** -0.5) * DN_BETA,
        'w_o': jax.random.normal(ks[17], (DEPTH, D_MODEL, D_MODEL), f32) * (D_MODEL ** -0.5) * DN_BETA,
        'ln_g': 1.0 + 0.01 * jax.random.normal(ks[18], (DEPTH, D_MODEL), f32),
        'ln_b': 0.01 * jax.random.normal(ks[19], (DEPTH, D_MODEL), f32),
    }


def reference(x_prompt, x_sample, state_conv, state_lru, cache_k_win, cache_v_win, w_in, conv_w, conv_b,
              w_gate_a, b_gate_a, w_gate_x, b_gate_x, lru_lambda, sinks, w_out_rnn, w_out_attn, w_o, ln_g, ln_b):
    y_p, y_s = x_prompt, x_sample
    conv_p, lru_p, kw_p, vw_p = [], [], [], []
    conv_s, lru_s, kw_s, vw_s = [], [], [], []
    for l in range(DEPTH):
        y_p, c1, h1, k1, v1 = _prompt_layer(y_p, w_in[l], conv_w[l], conv_b[l], w_gate_a[l], b_gate_a[l],
                                            w_gate_x[l], b_gate_x[l], lru_lambda[l], sinks[l],
                                            w_out_rnn[l], w_out_attn[l], w_o[l], ln_g[l], ln_b[l])
        y_s, c2, h2, k2, v2 = _sample_layer(y_s, state_conv[l], state_lru[l], cache_k_win[l], cache_v_win[l],
                                            w_in[l], conv_w[l], conv_b[l], w_gate_a[l], b_gate_a[l],
                                            w_gate_x[l], b_gate_x[l], lru_lambda[l], sinks[l],
                                            w_out_rnn[l], w_out_attn[l], w_o[l], ln_g[l], ln_b[l])
        conv_p.append(c1); lru_p.append(h1); kw_p.append(k1); vw_p.append(v1)
        conv_s.append(c2); lru_s.append(h2); kw_s.append(k2); vw_s.append(v2)
    conv_prompt = jnp.stack(conv_p)
    lru_prompt = jnp.stack(lru_p)
    k_win_prompt = jnp.stack(kw_p)
    v_win_prompt = jnp.stack(vw_p)
    conv_sample = jnp.stack(conv_s)
    lru_sample = jnp.stack(lru_s)
    k_win_sample = jnp.stack(kw_s)
    v_win_sample = jnp.stack(vw_s)
    return (y_p, y_s, conv_prompt, lru_prompt, k_win_prompt, v_win_prompt,
            conv_sample, lru_sample, k_win_sample, v_win_sample)
```

```python
import functools

import jax
import jax.numpy as jnp
from jax import lax
from jax.experimental import pallas as pl
from jax.experimental.pallas import tpu as pltpu

F32 = jnp.float32
BF16 = jnp.bfloat16

D_MODEL = 4096
BATCH = 2
SEQ = 4096
DEC_BATCH = 128
PAST_LEN = 8192
D_RNN = D_MODEL
LRU_BLOCK = 256
LRU_BLOCKS = D_RNN // LRU_BLOCK
CONV_W = 4
LRU_C = 8.0
N_HEADS = 32
N_KV = 8
HEAD_DIM = 128
GROUP = N_HEADS // N_KV
D_ATTN = N_HEADS * HEAD_DIM
D_KV = N_KV * HEAD_DIM
WINDOW = 128
ROPE_THETA = 10000.0
LN_EPS = 1e-5
DN_ALPHA = 2.0 ** 0.25
NEG_BIG = -1e30
N_COLS = 2 * D_RNN + 2 * D_ATTN + 2 * D_KV + 2 * D_MODEL

M_PROMPT = BATCH * SEQ
M_ALL = M_PROMPT + DEC_BATCH

COL_U = 0
COL_GRNN = D_RNN
COL_Q = 2 * D_RNN
COL_K = COL_Q + D_ATTN
COL_V = COL_K + D_KV
COL_GATTN = COL_V + D_KV
COL_MRNN = COL_GATTN + D_ATTN
COL_MATTN = COL_MRNN + D_MODEL

LANES = 128
SUBLANES = 8
MIB = 1024 * 1024

TM_A = 832
TN_A = 1024
TT_R = 512
TC_R = 512
TM_M = 832
TN_M = 512
TM_O = 128
BB_S = 8


def _cparams(sem, vmem_mib):
    return pltpu.CompilerParams(dimension_semantics=sem, vmem_limit_bytes=vmem_mib * MIB)


def _inproj_kernel(x_ref, w_ref, cos_ref, sin_ref, o_ref, acc_ref):
    n = pl.program_id(0)
    acc_ref[...] = jnp.dot(x_ref[...], w_ref[...], preferred_element_type=F32)

    def seg(lo, hi):
        return (n >= lo // TN_A) & (n < hi // TN_A)

    is_silu = seg(COL_GRNN, COL_Q) | seg(COL_GATTN, COL_MRNN)
    is_sig = seg(COL_MRNN, N_COLS)
    is_rope = seg(COL_Q, COL_V)
    is_plain = seg(COL_U, COL_GRNN) | seg(COL_V, COL_GATTN)

    @pl.when(is_plain)
    def _():
        o_ref[...] = acc_ref[...].astype(BF16)

    @pl.when(is_silu)
    def _():
        a = acc_ref[...]
        o_ref[...] = (a * jax.nn.sigmoid(a)).astype(BF16)

    @pl.when(is_sig)
    def _():
        o_ref[...] = jax.nn.sigmoid(acc_ref[...]).astype(BF16)

    @pl.when(is_rope)
    def _():
        c = cos_ref[...]
        s = sin_ref[...]
        for h in range(TN_A // HEAD_DIM):
            sl = slice(h * HEAD_DIM, (h + 1) * HEAD_DIM)
            xh = acc_ref[:, sl]
            o_ref[:, sl] = (xh * c + pltpu.roll(xh, HEAD_DIM // 2, axis=1) * s).astype(BF16)


def _inproj(x_bf, w_bf, cos_t, sin_t):
    grid = (N_COLS // TN_A, M_ALL // TM_A)
    return pl.pallas_call(
        _inproj_kernel,
        grid=grid,
        in_specs=[
            pl.BlockSpec((TM_A, D_MODEL), lambda n, m: (m, 0)),
            pl.BlockSpec((D_MODEL, TN_A), lambda n, m: (0, n)),
            pl.BlockSpec((TM_A, HEAD_DIM), lambda n, m: (m, 0)),
            pl.BlockSpec((TM_A, HEAD_DIM), lambda n, m: (m, 0)),
        ],
        out_specs=pl.BlockSpec((TM_A, TN_A), lambda n, m: (m, n)),
        out_shape=jax.ShapeDtypeStruct((M_ALL, N_COLS), BF16),
        scratch_shapes=[pltpu.VMEM((TM_A, TN_A), F32)],
        compiler_params=_cparams(("arbitrary", "arbitrary"), 48),
        name="inproj",
    )(x_bf, w_bf, cos_t, sin_t)


def _softplus(x):
    return jnp.maximum(x, 0.0) + jnp.log1p(jnp.exp(-jnp.abs(x)))


def _lru_coeffs(xc, wa_ref, ba, wx_ref, bx, sp, blk):
    xb = xc.astype(BF16)
    r = jax.nn.sigmoid(jnp.dot(xb, wa_ref[blk], preferred_element_type=F32) + ba)
    i = jax.nn.sigmoid(jnp.dot(xb, wx_ref[blk], preferred_element_type=F32) + bx)
    a = jnp.exp(-LRU_C * r * sp)
    b = jnp.sqrt(1.0 - a * a) * i * xc
    return a, b


def _group_scan(a, b, sub):
    for s in (1, 2, 4):
        m = sub >= s
        a_sh = jnp.where(m, pltpu.roll(a, s, axis=0), 1.0)
        b_sh = jnp.where(m, pltpu.roll(b, s, axis=0), 0.0)
        b = a * b_sh + b
        a = a * a_sh
    return a, b


def _rnn_prompt_kernel(u_ref, g_ref, cw_ref, cb_ref, wa_ref, ba_ref, wx_ref, bx_ref, lam_ref,
                       y_ref, conv_ref, hl_ref, ubuf, a_s, b_s, hcar):
    t = pl.program_id(2)

    @pl.when(t == 0)
    def _():
        ubuf[0:SUBLANES, :] = jnp.zeros((SUBLANES, TC_R), F32)
        hcar[...] = jnp.zeros((1, TC_R), F32)

    ubuf[SUBLANES:SUBLANES + TT_R, :] = u_ref[...].astype(F32)
    sp = _softplus(-lam_ref[...])
    sub = lax.broadcasted_iota(jnp.int32, (TT_R, LRU_BLOCK), 0) % SUBLANES

    for blk in range(TC_R // LRU_BLOCK):
        cs = slice(blk * LRU_BLOCK, (blk + 1) * LRU_BLOCK)
        xc = cb_ref[:, cs] + cw_ref[3:4, cs] * ubuf[SUBLANES:SUBLANES + TT_R, cs]
        for j in range(CONV_W - 1):
            off = SUBLANES - (CONV_W - 1) + j
            xc = xc + cw_ref[j:j + 1, cs] * ubuf[off:off + TT_R, cs]
        a, b = _lru_coeffs(xc, wa_ref, ba_ref[:, cs], wx_ref, bx_ref[:, cs], sp[:, cs], blk)
        a, b = _group_scan(a, b, sub)
        a_s[:, cs] = a
        b_s[:, cs] = b

    c = hcar[...]
    for j in range(TT_R // SUBLANES):
        rs = slice(j * SUBLANES, (j + 1) * SUBLANES)
        h = a_s[rs, :] * c + b_s[rs, :]
        b_s[rs, :] = h
        c = h[SUBLANES - 1:SUBLANES, :]
    hcar[...] = c

    y_ref[...] = (b_s[...] * g_ref[...].astype(F32)).astype(BF16)
    ubuf[0:SUBLANES, :] = ubuf[TT_R:TT_R + SUBLANES, :]
    conv_ref[0] = ubuf[SUBLANES - (CONV_W - 1):SUBLANES, :]
    hl_ref[0] = c


def _rnn_prompt(z, cw, cb, wa, ba, wx, bx, lam):
    nt = SEQ // TT_R
    nc = D_RNN // TC_R
    grid = (nc, BATCH, nt)
    vec = lambda rows: pl.BlockSpec((rows, TC_R), lambda c, b, t: (0, c))
    gw = pl.BlockSpec((TC_R // LRU_BLOCK, LRU_BLOCK, LRU_BLOCK), lambda c, b, t: (c, 0, 0))
    return pl.pallas_call(
        _rnn_prompt_kernel,
        grid=grid,
        in_specs=[
            pl.BlockSpec((TT_R, TC_R), lambda c, b, t: (b * nt + t, COL_U // TC_R + c)),
            pl.BlockSpec((TT_R, TC_R), lambda c, b, t: (b * nt + t, COL_GRNN // TC_R + c)),
            vec(CONV_W), vec(1), gw, vec(1), gw, vec(1), vec(1),
        ],
        out_specs=[
            pl.BlockSpec((TT_R, TC_R), lambda c, b, t: (b * nt + t, c)),
            pl.BlockSpec((1, CONV_W - 1, TC_R), lambda c, b, t: (b, 0, c)),
            pl.BlockSpec((1, 1, TC_R), lambda c, b, t: (b, 0, c)),
        ],
        out_shape=[
            jax.ShapeDtypeStruct((M_ALL, D_RNN), BF16),
            jax.ShapeDtypeStruct((BATCH, CONV_W - 1, D_RNN), F32),
            jax.ShapeDtypeStruct((BATCH, 1, D_RNN), F32),
        ],
        scratch_shapes=[
            pltpu.VMEM((TT_R + 2 * SUBLANES, TC_R), F32),
            pltpu.VMEM((TT_R, TC_R), F32),
            pltpu.VMEM((TT_R, TC_R), F32),
            pltpu.VMEM((1, TC_R), F32),
        ],
        compiler_params=_cparams(("arbitrary", "arbitrary", "arbitrary"), 32),
        name="rnn_prompt",
    )(z, z, cw, cb, wa, ba, wx, bx, lam)


def _rnn_sample_kernel(u_ref, g_ref, sc_ref, h0_ref, cw_ref, cb_ref, wa_ref, ba_ref, wx_ref, bx_ref,
                       lam_ref, y_ref, h_ref):
    sp = _softplus(-lam_ref[...])
    for blk in range(TC_R // LRU_BLOCK):
        cs = slice(blk * LRU_BLOCK, (blk + 1) * LRU_BLOCK)
        xc = cb_ref[:, cs] + cw_ref[3:4, cs] * u_ref[:, cs]
        for j in range(CONV_W - 1):
            xc = xc + cw_ref[j:j + 1, cs] * sc_ref[j, :, cs]
        a, b = _lru_coeffs(xc, wa_ref, ba_ref[:, cs], wx_ref, bx_ref[:, cs], sp[:, cs], blk)
        h = a * h0_ref[:, cs] + b
        h_ref[:, cs] = h
        y_ref[:, cs] = (h * g_ref[:, cs]).astype(BF16)


def _rnn_sample(u_s, g_s, sc_t, h0, cw, cb, wa, ba, wx, bx, lam):
    nc = D_RNN // TC_R
    vec = lambda rows: pl.BlockSpec((rows, TC_R), lambda c: (0, c))
    gw = pl.BlockSpec((TC_R // LRU_BLOCK, LRU_BLOCK, LRU_BLOCK), lambda c: (c, 0, 0))
    return pl.pallas_call(
        _rnn_sample_kernel,
        grid=(nc,),
        in_specs=[
            vec(DEC_BATCH), vec(DEC_BATCH),
            pl.BlockSpec((CONV_W - 1, DEC_BATCH, TC_R), lambda c: (0, 0, c)),
            vec(DEC_BATCH), vec(CONV_W), vec(1), gw, vec(1), gw, vec(1), vec(1),
        ],
        out_specs=[vec(DEC_BATCH), vec(DEC_BATCH)],
        out_shape=[
            jax.ShapeDtypeStruct((DEC_BATCH, D_RNN), BF16),
            jax.ShapeDtypeStruct((DEC_BATCH, D_RNN), F32),
        ],
        compiler_params=_cparams(("arbitrary",), 32),
        name="rnn_sample",
    )(u_s, g_s, sc_t, h0, cw, cb, wa, ba, wx, bx, lam)


_SCALE = HEAD_DIM ** -0.5
_TRANS_B = (((1,), (1,)), ((), ()))


def _attn_prompt_kernel(q_ref, kp_ref, kc_ref, vp_ref, vc_ref, glo_ref, ghi_ref, sink_ref, o_ref):
    i = pl.program_id(1)
    rows = GROUP * WINDOW
    r = lax.broadcasted_iota(jnp.int32, (rows, 2 * WINDOW), 0) % WINDOW
    c = lax.broadcasted_iota(jnp.int32, (rows, 2 * WINDOW), 1)
    cmin = jnp.where(i > 0, 0, WINDOW)
    vis = (c > r) & (c <= r + WINDOW) & (c >= cmin)
    rgrp = lax.broadcasted_iota(jnp.int32, (rows, 1), 0) // WINDOW
    for h in range(N_KV):
        ks = slice(h * HEAD_DIM, (h + 1) * HEAD_DIM)
        qh = jnp.concatenate(
            [q_ref[:, (GROUP * h + g) * HEAD_DIM:(GROUP * h + g + 1) * HEAD_DIM] for g in range(GROUP)], axis=0)
        kb = jnp.concatenate([kp_ref[:, ks], kc_ref[:, ks]], axis=0)
        vb = jnp.concatenate([vp_ref[:, ks], vc_ref[:, ks]], axis=0)
        s = lax.dot_general(qh, kb, _TRANS_B, preferred_element_type=F32) * _SCALE
        s = jnp.where(vis, s, NEG_BIG)
        sink = jnp.zeros((rows, 1), F32)
        for g in range(GROUP):
            sink = jnp.where(rgrp == g, sink_ref[GROUP * h + g], sink)
        m = jnp.maximum(jnp.max(s, axis=-1, keepdims=True), sink)
        p = jnp.exp(s - m)
        denom = jnp.sum(p, axis=-1, keepdims=True) + jnp.exp(sink - m)
        o = jnp.dot(p.astype(BF16), vb, preferred_element_type=F32) / denom
        for g in range(GROUP):
            col = (GROUP * h + g) * HEAD_DIM
            gref, gcol = (glo_ref, col) if col < D_ATTN // 2 else (ghi_ref, col - D_ATTN // 2)
            gate = gref[:, gcol:gcol + HEAD_DIM].astype(F32)
            og = o[g * WINDOW:(g + 1) * WINDOW, :] * gate
            o_ref[:, col:col + HEAD_DIM] = og.astype(BF16)


def _attn_prompt(z, sinks):
    nb = SEQ // WINDOW
    half = D_ATTN // 2
    row = lambda b, i: b * nb + i
    prev = lambda b, i: b * nb + jnp.maximum(i - 1, 0)
    return pl.pallas_call(
        _attn_prompt_kernel,
        grid=(BATCH, nb),
        in_specs=[
            pl.BlockSpec((WINDOW, D_ATTN), lambda b, i: (row(b, i), COL_Q // D_ATTN)),
            pl.BlockSpec((WINDOW, D_KV), lambda b, i: (prev(b, i), COL_K // D_KV)),
            pl.BlockSpec((WINDOW, D_KV), lambda b, i: (row(b, i), COL_K // D_KV)),
            pl.BlockSpec((WINDOW, D_KV), lambda b, i: (prev(b, i), COL_V // D_KV)),
            pl.BlockSpec((WINDOW, D_KV), lambda b, i: (row(b, i), COL_V // D_KV)),
            pl.BlockSpec((WINDOW, half), lambda b, i: (row(b, i), COL_GATTN // half)),
            pl.BlockSpec((WINDOW, half), lambda b, i: (row(b, i), COL_GATTN // half + 1)),
            pl.BlockSpec(memory_space=pltpu.SMEM),
        ],
        out_specs=pl.BlockSpec((WINDOW, D_ATTN), lambda b, i: (row(b, i), 0)),
        out_shape=jax.ShapeDtypeStruct((M_ALL, D_ATTN), BF16),
        compiler_params=_cparams(("arbitrary", "arbitrary"), 32),
        name="attn_prompt",
    )(z, z, z, z, z, z, z, sinks)


def _attn_sample_kernel(q_ref, g_ref, ks_ref, vs_ref, ck_ref, cv_ref, sink_ref, o_ref, kw_ref, vw_ref):
    head = lax.broadcasted_iota(jnp.int32, (N_HEADS, WINDOW), 0) // GROUP
    head1 = lax.broadcasted_iota(jnp.int32, (N_HEADS, 1), 0) // GROUP
    col = lax.broadcasted_iota(jnp.int32, (N_HEADS, WINDOW), 1)
    sink = sink_ref[...]

    def body(bi, carry):
        qb = q_ref[bi]
        qb16 = qb.astype(BF16)
        s = jnp.zeros((N_HEADS, WINDOW), F32)
        sn = jnp.zeros((N_HEADS, 1), F32)
        for h in range(N_KV):
            ks = slice(h * HEAD_DIM, (h + 1) * HEAD_DIM)
            kh = ck_ref[bi, :, ks].astype(BF16)
            sh = lax.dot_general(qb16, kh, _TRANS_B, preferred_element_type=F32)
            s = jnp.where(head == h, sh, s)
            snh = jnp.sum(qb * ks_ref[bi, :, ks], axis=-1, keepdims=True)
            sn = jnp.where(head1 == h, snh, sn)
        s = jnp.where(col >= 1, s * _SCALE, NEG_BIG)
        sn = sn * _SCALE
        m = jnp.maximum(jnp.maximum(jnp.max(s, axis=-1, keepdims=True), sn), sink)
        p = jnp.exp(s - m)
        pn = jnp.exp(sn - m)
        denom = jnp.sum(p, axis=-1, keepdims=True) + pn + jnp.exp(sink - m)
        p16 = p.astype(BF16)
        o = jnp.zeros((N_HEADS, HEAD_DIM), F32)
        for h in range(N_KV):
            ks = slice(h * HEAD_DIM, (h + 1) * HEAD_DIM)
            vh = cv_ref[bi, :, ks].astype(BF16)
            oh = jnp.dot(p16, vh, preferred_element_type=F32) + pn * vs_ref[bi, :, ks]
            o = jnp.where(head == h, oh, o)
        o_ref[bi] = o / denom * g_ref[bi]
        kw_ref[bi, 0:WINDOW - 1, :] = ck_ref[bi, 1:WINDOW, :]
        kw_ref[bi, WINDOW - 1:WINDOW, :] = ks_ref[bi]
        vw_ref[bi, 0:WINDOW - 1, :] = cv_ref[bi, 1:WINDOW, :]
        vw_ref[bi, WINDOW - 1:WINDOW, :] = vs_ref[bi]
        return carry

    lax.fori_loop(0, BB_S, body, 0)


def _attn_sample(q_s, g_s, k_s, v_s, ck, cv, sinks_col):
    hb = pl.BlockSpec((BB_S, N_HEADS, HEAD_DIM), lambda s: (s, 0, 0))
    rb = pl.BlockSpec((BB_S, 1, D_KV), lambda s: (s, 0, 0))
    cb = pl.BlockSpec((BB_S, WINDOW, D_KV), lambda s: (s, 0, 0))
    return pl.pallas_call(
        _attn_sample_kernel,
        grid=(DEC_BATCH // BB_S,),
        in_specs=[hb, hb, rb, rb, cb, cb, pl.BlockSpec((N_HEADS, 1), lambda s: (0, 0))],
        out_specs=[hb, cb, cb],
        out_shape=[
            jax.ShapeDtypeStruct((DEC_BATCH, N_HEADS, HEAD_DIM), F32),
            jax.ShapeDtypeStruct((DEC_BATCH, WINDOW, D_KV), F32),
            jax.ShapeDtypeStruct((DEC_BATCH, WINDOW, D_KV), F32),
        ],
        compiler_params=_cparams(("arbitrary",), 48),
        name="attn_sample",
    )(q_s, g_s, k_s, v_s, ck, cv, sinks_col)


def _merge_kernel(y_ref, og_ref, wr_ref, wa_ref, mr_ref, ma_ref, o_ref):
    br = jnp.dot(y_ref[...], wr_ref[...], preferred_element_type=F32)
    ba = jnp.dot(og_ref[...], wa_ref[...], preferred_element_type=F32)
    o_ref[...] = (mr_ref[...].astype(F32) * br + ma_ref[...].astype(F32) * ba).astype(BF16)


def _merge(y_all, og_all, wr, wa, z):
    grid = (M_ALL // TM_M, D_MODEL // TN_M)
    act = pl.BlockSpec((TM_M, D_MODEL), lambda m, n: (m, 0))
    wsp = pl.BlockSpec((D_MODEL, TN_M), lambda m, n: (0, n))
    return pl.pallas_call(
        _merge_kernel,
        grid=grid,
        in_specs=[
            act, act, wsp, wsp,
            pl.BlockSpec((TM_M, TN_M), lambda m, n: (m, COL_MRNN // TN_M + n)),
            pl.BlockSpec((TM_M, TN_M), lambda m, n: (m, COL_MATTN // TN_M + n)),
        ],
        out_specs=pl.BlockSpec((TM_M, TN_M), lambda m, n: (m, n)),
        out_shape=jax.ShapeDtypeStruct((M_ALL, D_MODEL), BF16),
        compiler_params=_cparams(("arbitrary", "arbitrary"), 56),
        name="merge",
    )(y_all, og_all, wr, wa, z, z)


def _out_kernel(m_ref, wo_ref, xp_ref, xs_ref, g_ref, b_ref, yp_ref, ys_ref):
    i = pl.program_id(0)
    acc = jnp.dot(m_ref[...], wo_ref[...], preferred_element_type=F32)

    def norm(x):
        r = DN_ALPHA * x + acc
        mu = jnp.mean(r, axis=-1, keepdims=True)
        d = r - mu
        var = jnp.mean(d * d, axis=-1, keepdims=True)
        return d * lax.rsqrt(var + LN_EPS) * g_ref[...] + b_ref[...]

    @pl.when(i < M_PROMPT // TM_O)
    def _():
        yp_ref[...] = norm(xp_ref[...])

    @pl.when(i == M_PROMPT // TM_O)
    def _():
        ys_ref[...] = norm(xs_ref[...])


def _out_proj(merged, wo, xp, xs, ln_g, ln_b):
    last = M_PROMPT // TM_O - 1
    clamp = lambda i: (jnp.minimum(i, last), 0)
    const = lambda i: (0, 0)
    return pl.pallas_call(
        _out_kernel,
        grid=(M_ALL // TM_O,),
        in_specs=[
            pl.BlockSpec((TM_O, D_MODEL), lambda i: (i, 0)),
            pl.BlockSpec((D_MODEL, D_MODEL), const, pipeline_mode=pl.Buffered(1)),
            pl.BlockSpec((TM_O, D_MODEL), clamp),
            pl.BlockSpec((DEC_BATCH, D_MODEL), const),
            pl.BlockSpec((1, D_MODEL), const),
            pl.BlockSpec((1, D_MODEL), const),
        ],
        out_specs=[
            pl.BlockSpec((TM_O, D_MODEL), clamp),
            pl.BlockSpec((DEC_BATCH, D_MODEL), const),
        ],
        out_shape=[
            jax.ShapeDtypeStruct((M_PROMPT, D_MODEL), F32),
            jax.ShapeDtypeStruct((DEC_BATCH, D_MODEL), F32),
        ],
        compiler_params=_cparams(("arbitrary",), 56),
        name="out_proj",
    )(merged, wo, xp, xs, ln_g, ln_b)


def _rope_tables():
    half = HEAD_DIM // 2
    pos = jnp.concatenate([
        jnp.tile(jnp.arange(SEQ, dtype=jnp.int32), BATCH),
        jnp.full((DEC_BATCH,), PAST_LEN, jnp.int32),
    ]).astype(F32)
    inv = ROPE_THETA ** (-jnp.arange(half, dtype=F32) / half)
    ang = pos[:, None] * inv[None, :]
    cos, sin = jnp.cos(ang), jnp.sin(ang)
    return jnp.concatenate([cos, cos], axis=-1), jnp.concatenate([-sin, sin], axis=-1)


def kernel(x_prompt, x_sample, state_conv, state_lru, cache_k_win, cache_v_win, w_in, conv_w, conv_b,
           w_gate_a, b_gate_a, w_gate_x, b_gate_x, lru_lambda, sinks, w_out_rnn, w_out_attn, w_o, ln_g, ln_b):
    xp = x_prompt.reshape(M_PROMPT, D_MODEL)
    xs = x_sample.reshape(DEC_BATCH, D_MODEL)
    x_bf = jnp.concatenate([xp.astype(BF16), xs.astype(BF16)], axis=0)
    cos_t, sin_t = _rope_tables()

    z = _inproj(x_bf, w_in[0].astype(BF16), cos_t, sin_t)

    cw, cb = conv_w[0], conv_b
    wa, wx = w_gate_a[0].astype(BF16), w_gate_x[0].astype(BF16)
    ba, bx, lam = b_gate_a, b_gate_x, lru_lambda

    y_all, conv_p, lru_p = _rnn_prompt(z, cw, cb, wa, ba, wx, bx, lam)
    zs = z[M_PROMPT:]
    u_s = zs[:, COL_U:COL_U + D_RNN].astype(F32)
    g_s = zs[:, COL_GRNN:COL_GRNN + D_RNN].astype(F32)
    sc_t = jnp.transpose(state_conv[0], (1, 0, 2))
    y_s, lru_s = _rnn_sample(u_s, g_s, sc_t, state_lru[0], cw, cb, wa, ba, wx, bx, lam)
    y_all = lax.dynamic_update_slice(y_all, y_s, (M_PROMPT, 0))
    conv_s = jnp.stack([state_conv[0, :, 1], state_conv[0, :, 2], u_s], axis=1)

    og_all = _attn_prompt(z, sinks[0])
    q_s = zs[:, COL_Q:COL_Q + D_ATTN].astype(F32).reshape(DEC_BATCH, N_HEADS, HEAD_DIM)
    ga_s = zs[:, COL_GATTN:COL_GATTN + D_ATTN].astype(F32).reshape(DEC_BATCH, N_HEADS, HEAD_DIM)
    k_s = zs[:, COL_K:COL_K + D_KV].astype(F32).reshape(DEC_BATCH, 1, D_KV)
    v_s = zs[:, COL_V:COL_V + D_KV].astype(F32).reshape(DEC_BATCH, 1, D_KV)
    ck = cache_k_win[0].reshape(DEC_BATCH, WINDOW, D_KV)
    cv = cache_v_win[0].reshape(DEC_BATCH, WINDOW, D_KV)
    og_s, kw_s, vw_s = _attn_sample(q_s, ga_s, k_s, v_s, ck, cv, sinks[0].reshape(N_HEADS, 1))
    og_all = lax.dynamic_update_slice(og_all, og_s.reshape(DEC_BATCH, D_ATTN).astype(BF16), (M_PROMPT, 0))

    merged = _merge(y_all, og_all, w_out_rnn[0].astype(BF16), w_out_attn[0].astype(BF16), z)
    y_p, y_smp = _out_proj(merged, w_o[0].astype(BF16), xp, xs, ln_g, ln_b)

    zp = z[:M_PROMPT].reshape(BATCH, SEQ, N_COLS)
    kw_p = zp[:, SEQ - WINDOW:, COL_K:COL_K + D_KV].astype(F32).reshape(BATCH, WINDOW, N_KV, HEAD_DIM)
    vw_p = zp[:, SEQ - WINDOW:, COL_V:COL_V + D_KV].astype(F32).reshape(BATCH, WINDOW, N_KV, HEAD_DIM)

    return (
        y_p.reshape(BATCH, SEQ, D_MODEL),
        y_smp.reshape(DEC_BATCH, 1, D_MODEL),
        conv_p[None],
        lru_p.reshape(1, BATCH, D_RNN),
        kw_p[None],
        vw_p[None],
        conv_s[None],
        lru_s[None],
        kw_s.reshape(1, DEC_BATCH, WINDOW, N_KV, HEAD_DIM),
        vw_s.reshape(1, DEC_BATCH, WINDOW, N_KV, HEAD_DIM),
    )
```

```python
import functools

import jax
import jax.numpy as jnp
from jax import lax
from jax.experimental import pallas as pl
from jax.experimental.pallas import tpu as pltpu

F32 = jnp.float32
BF16 = jnp.bfloat16

D_MODEL = 4096
BATCH = 2
SEQ = 4096
DEC_BATCH = 128
PAST_LEN = 8192
D_RNN = D_MODEL
LRU_BLOCK = 256
LRU_BLOCKS = D_RNN // LRU_BLOCK
CONV_W = 4
LRU_C = 8.0
N_HEADS = 32
N_KV = 8
HEAD_DIM = 128
GROUP = N_HEADS // N_KV
D_ATTN = N_HEADS * HEAD_DIM
D_KV = N_KV * HEAD_DIM
WINDOW = 128
ROPE_THETA = 10000.0
LN_EPS = 1e-5
DN_ALPHA = 2.0 ** 0.25
NEG_BIG = -1e30
N_COLS = 2 * D_RNN + 2 * D_ATTN + 2 * D_KV + 2 * D_MODEL

M_PROMPT = BATCH * SEQ
M_ALL = M_PROMPT + DEC_BATCH

COL_U = 0
COL_GRNN = D_RNN
COL_Q = 2 * D_RNN
COL_K = COL_Q + D_ATTN
COL_V = COL_K + D_KV
COL_GATTN = COL_V + D_KV
COL_MRNN = COL_GATTN + D_ATTN
COL_MATTN = COL_MRNN + D_MODEL

LANES = 128
SUBLANES = 8
MIB = 1024 * 1024

TM_A = 832
TN_A = 1024
CK_A = 512
NCK_A = D_MODEL // CK_A
TT_R = 512
TC_R = 512
RB_W = 128
NBLK_W = D_MODEL // RB_W
TM_M = 1024
TN_M = 256
TM_O = 128
BB_S = 8


def _cparams(sem, vmem_mib):
    return pltpu.CompilerParams(dimension_semantics=sem, vmem_limit_bytes=vmem_mib * MIB)


def _w_chunk_copy(w_hbm, stage, sem, n, c, slot):
    return pltpu.make_async_copy(
        w_hbm.at[pl.ds(c * CK_A, CK_A), pl.ds(n * TN_A, TN_A)], stage.at[slot], sem.at[slot])


def _inproj_kernel(x_ref, w_hbm, cos_ref, sin_ref, o_ref, acc_ref, wbf, stage, sem):
    n = pl.program_id(0)
    m = pl.program_id(1)
    has_next = n + 1 < pl.num_programs(0)
    cur = n % 2

    @pl.when((n == 0) & (m == 0))
    def _():
        for c in range(NCK_A):
            cp = _w_chunk_copy(w_hbm, stage, sem, 0, c, c % 2)
            cp.start()
            cp.wait()
            wbf[0, c] = stage[c % 2].astype(BF16)

    @pl.when(has_next & (m >= 1) & (m <= NCK_A))
    def _():
        c = m - 1
        _w_chunk_copy(w_hbm, stage, sem, n + 1, c, c % 2).wait()
        wbf[1 - cur, c] = stage[c % 2].astype(BF16)

    @pl.when(has_next & (m < NCK_A))
    def _():
        _w_chunk_copy(w_hbm, stage, sem, n + 1, m, m % 2).start()

    acc_ref[...] = jnp.dot(x_ref[...], wbf[cur].reshape(D_MODEL, TN_A), preferred_element_type=F32)

    def seg(lo, hi):
        return (n >= lo // TN_A) & (n < hi // TN_A)

    is_silu = seg(COL_GRNN, COL_Q) | seg(COL_GATTN, COL_MRNN)
    is_sig = seg(COL_MRNN, N_COLS)
    is_rope = seg(COL_Q, COL_V)
    is_plain = seg(COL_U, COL_GRNN) | seg(COL_V, COL_GATTN)

    @pl.when(is_plain)
    def _():
        o_ref[...] = acc_ref[...].astype(BF16)

    @pl.when(is_silu)
    def _():
        a = acc_ref[...]
        o_ref[...] = (a * jax.nn.sigmoid(a)).astype(BF16)

    @pl.when(is_sig)
    def _():
        o_ref[...] = jax.nn.sigmoid(acc_ref[...]).astype(BF16)

    @pl.when(is_rope)
    def _():
        c = cos_ref[...]
        s = sin_ref[...]
        for h in range(TN_A // HEAD_DIM):
            sl = slice(h * HEAD_DIM, (h + 1) * HEAD_DIM)
            xh = acc_ref[:, sl]
            o_ref[:, sl] = (xh * c + pltpu.roll(xh, HEAD_DIM // 2, axis=1) * s).astype(BF16)


def _inproj(x_bf, w_f32, cos_t, sin_t):
    grid = (N_COLS // TN_A, M_ALL // TM_A)
    assert grid[1] >= NCK_A + 1
    return pl.pallas_call(
        _inproj_kernel,
        grid=grid,
        in_specs=[
            pl.BlockSpec((TM_A, D_MODEL), lambda n, m: (m, 0)),
            pl.BlockSpec(memory_space=pl.ANY),
            pl.BlockSpec((TM_A, HEAD_DIM), lambda n, m: (m, 0)),
            pl.BlockSpec((TM_A, HEAD_DIM), lambda n, m: (m, 0)),
        ],
        out_specs=pl.BlockSpec((TM_A, TN_A), lambda n, m: (m, n)),
        out_shape=jax.ShapeDtypeStruct((M_ALL, N_COLS), BF16),
        scratch_shapes=[
            pltpu.VMEM((TM_A, TN_A), F32),
            pltpu.VMEM((2, NCK_A, CK_A, TN_A), BF16),
            pltpu.VMEM((2, CK_A, TN_A), F32),
            pltpu.SemaphoreType.DMA((2,)),
        ],
        compiler_params=_cparams(("arbitrary", "arbitrary"), 52),
        name="inproj",
    )(x_bf, w_f32, cos_t, sin_t)


def _softplus(x):
    return jnp.maximum(x, 0.0) + jnp.log1p(jnp.exp(-jnp.abs(x)))


def _lru_coeffs(xc, wa_ref, ba, wx_ref, bx, sp, blk):
    xb = xc.astype(BF16)
    r = jax.nn.sigmoid(jnp.dot(xb, wa_ref[blk], preferred_element_type=F32) + ba)
    i = jax.nn.sigmoid(jnp.dot(xb, wx_ref[blk], preferred_element_type=F32) + bx)
    a = jnp.exp(-LRU_C * r * sp)
    b = jnp.sqrt(1.0 - a * a) * i * xc
    return a, b


def _group_scan(a, b, sub):
    for s in (1, 2, 4):
        m = sub >= s
        a_sh = jnp.where(m, pltpu.roll(a, s, axis=0), 1.0)
        b_sh = jnp.where(m, pltpu.roll(b, s, axis=0), 0.0)
        b = a * b_sh + b
        a = a * a_sh
    return a, b


def _rnn_prompt_kernel(u_ref, g_ref, cw_ref, cb_ref, wa_ref, ba_ref, wx_ref, bx_ref, lam_ref,
                       w0_ref, w1_ref, w2_ref,
                       y_ref, conv_ref, hl_ref, w0_out, w1_out, w2_out, ubuf, a_s, b_s, hcar):
    t = pl.program_id(2)

    step = (pl.program_id(0) * pl.num_programs(1) + pl.program_id(1)) * pl.num_programs(2) + t
    for k, (src, dst) in enumerate(((w0_ref, w0_out), (w1_ref, w1_out), (w2_ref, w2_out))):
        @pl.when((step >= k * NBLK_W) & (step < (k + 1) * NBLK_W))
        def _(src=src, dst=dst):
            dst[...] = src[...].astype(BF16)

    @pl.when(t == 0)
    def _():
        ubuf[0:SUBLANES, :] = jnp.zeros((SUBLANES, TC_R), F32)
        hcar[...] = jnp.zeros((1, TC_R), F32)

    ubuf[SUBLANES:SUBLANES + TT_R, :] = u_ref[...].astype(F32)
    sp = _softplus(-lam_ref[...])
    sub = lax.broadcasted_iota(jnp.int32, (TT_R, LRU_BLOCK), 0) % SUBLANES

    for blk in range(TC_R // LRU_BLOCK):
        cs = slice(blk * LRU_BLOCK, (blk + 1) * LRU_BLOCK)
        xc = cb_ref[:, cs] + cw_ref[3:4, cs] * ubuf[SUBLANES:SUBLANES + TT_R, cs]
        for j in range(CONV_W - 1):
            off = SUBLANES - (CONV_W - 1) + j
            xc = xc + cw_ref[j:j + 1, cs] * ubuf[off:off + TT_R, cs]
        a, b = _lru_coeffs(xc, wa_ref, ba_ref[:, cs], wx_ref, bx_ref[:, cs], sp[:, cs], blk)
        a, b = _group_scan(a, b, sub)
        a_s[:, cs] = a
        b_s[:, cs] = b

    c = hcar[...]
    for j in range(TT_R // SUBLANES):
        rs = slice(j * SUBLANES, (j + 1) * SUBLANES)
        h = a_s[rs, :] * c + b_s[rs, :]
        b_s[rs, :] = h
        c = h[SUBLANES - 1:SUBLANES, :]
    hcar[...] = c

    y_ref[...] = (b_s[...] * g_ref[...].astype(F32)).astype(BF16)
    ubuf[0:SUBLANES, :] = ubuf[TT_R:TT_R + SUBLANES, :]
    conv_ref[0] = ubuf[SUBLANES - (CONV_W - 1):SUBLANES, :]
    hl_ref[0] = c


def _rnn_prompt(z, cw, cb, wa, ba, wx, bx, lam, w_side):
    nt = SEQ // TT_R
    nc = D_RNN // TC_R
    grid = (nc, BATCH, nt)
    assert len(w_side) * NBLK_W <= nc * BATCH * nt
    vec = lambda rows: pl.BlockSpec((rows, TC_R), lambda c, b, t: (0, c))
    gw = pl.BlockSpec((TC_R // LRU_BLOCK, LRU_BLOCK, LRU_BLOCK), lambda c, b, t: (c, 0, 0))

    def wblk(k):
        def imap(c, b, t):
            step = (c * BATCH + b) * nt + t
            return (jnp.clip(step - k * NBLK_W, 0, NBLK_W - 1), 0)
        return pl.BlockSpec((RB_W, D_MODEL), imap)

    return pl.pallas_call(
        _rnn_prompt_kernel,
        grid=grid,
        in_specs=[
            pl.BlockSpec((TT_R, TC_R), lambda c, b, t: (b * nt + t, COL_U // TC_R + c)),
            pl.BlockSpec((TT_R, TC_R), lambda c, b, t: (b * nt + t, COL_GRNN // TC_R + c)),
            vec(CONV_W), vec(1), gw, vec(1), gw, vec(1), vec(1),
            wblk(0), wblk(1), wblk(2),
        ],
        out_specs=[
            pl.BlockSpec((TT_R, TC_R), lambda c, b, t: (b * nt + t, c)),
            pl.BlockSpec((1, CONV_W - 1, TC_R), lambda c, b, t: (b, 0, c)),
            pl.BlockSpec((1, 1, TC_R), lambda c, b, t: (b, 0, c)),
            wblk(0), wblk(1), wblk(2),
        ],
        out_shape=[
            jax.ShapeDtypeStruct((M_PROMPT, D_RNN), BF16),
            jax.ShapeDtypeStruct((BATCH, CONV_W - 1, D_RNN), F32),
            jax.ShapeDtypeStruct((BATCH, 1, D_RNN), F32),
        ] + [jax.ShapeDtypeStruct((D_MODEL, D_MODEL), BF16)] * 3,
        scratch_shapes=[
            pltpu.VMEM((TT_R + 2 * SUBLANES, TC_R), F32),
            pltpu.VMEM((TT_R, TC_R), F32),
            pltpu.VMEM((TT_R, TC_R), F32),
            pltpu.VMEM((1, TC_R), F32),
        ],
        compiler_params=_cparams(("arbitrary", "arbitrary", "arbitrary"), 48),
        name="rnn_prompt",
    )(z, z, cw, cb, wa, ba, wx, bx, lam, *w_side)


def _rnn_sample_kernel(u_ref, g_ref, sc_ref, h0_ref, cw_ref, cb_ref, wa_ref, ba_ref, wx_ref, bx_ref,
                       lam_ref, y_ref, h_ref):
    sp = _softplus(-lam_ref[...])
    for blk in range(TC_R // LRU_BLOCK):
        cs = slice(blk * LRU_BLOCK, (blk + 1) * LRU_BLOCK)
        xc = cb_ref[:, cs] + cw_ref[3:4, cs] * u_ref[:, cs]
        for j in range(CONV_W - 1):
            xc = xc + cw_ref[j:j + 1, cs] * sc_ref[j, :, cs]
        a, b = _lru_coeffs(xc, wa_ref, ba_ref[:, cs], wx_ref, bx_ref[:, cs], sp[:, cs], blk)
        h = a * h0_ref[:, cs] + b
        h_ref[:, cs] = h
        y_ref[:, cs] = (h * g_ref[:, cs]).astype(BF16)


def _rnn_sample(u_s, g_s, sc_t, h0, cw, cb, wa, ba, wx, bx, lam):
    nc = D_RNN // TC_R
    vec = lambda rows: pl.BlockSpec((rows, TC_R), lambda c: (0, c))
    gw = pl.BlockSpec((TC_R // LRU_BLOCK, LRU_BLOCK, LRU_BLOCK), lambda c: (c, 0, 0))
    return pl.pallas_call(
        _rnn_sample_kernel,
        grid=(nc,),
        in_specs=[
            vec(DEC_BATCH), vec(DEC_BATCH),
            pl.BlockSpec((CONV_W - 1, DEC_BATCH, TC_R), lambda c: (0, 0, c)),
            vec(DEC_BATCH), vec(CONV_W), vec(1), gw, vec(1), gw, vec(1), vec(1),
        ],
        out_specs=[vec(DEC_BATCH), vec(DEC_BATCH)],
        out_shape=[
            jax.ShapeDtypeStruct((DEC_BATCH, D_RNN), BF16),
            jax.ShapeDtypeStruct((DEC_BATCH, D_RNN), F32),
        ],
        compiler_params=_cparams(("arbitrary",), 32),
        name="rnn_sample",
    )(u_s, g_s, sc_t, h0, cw, cb, wa, ba, wx, bx, lam)


_SCALE = HEAD_DIM ** -0.5
_TRANS_B = (((1,), (1,)), ((), ()))


def _attn_prompt_kernel(q_ref, kp_ref, kc_ref, vp_ref, vc_ref, glo_ref, ghi_ref, sink_ref, o_ref):
    i = pl.program_id(1)
    rows = GROUP * WINDOW
    r = lax.broadcasted_iota(jnp.int32, (rows, 2 * WINDOW), 0) % WINDOW
    c = lax.broadcasted_iota(jnp.int32, (rows, 2 * WINDOW), 1)
    cmin = jnp.where(i > 0, 0, WINDOW)
    vis = (c > r) & (c <= r + WINDOW) & (c >= cmin)
    rgrp = lax.broadcasted_iota(jnp.int32, (rows, 1), 0) // WINDOW
    for h in range(N_KV):
        ks = slice(h * HEAD_DIM, (h + 1) * HEAD_DIM)
        qh = jnp.concatenate(
            [q_ref[:, (GROUP * h + g) * HEAD_DIM:(GROUP * h + g + 1) * HEAD_DIM] for g in range(GROUP)], axis=0)
        kb = jnp.concatenate([kp_ref[:, ks], kc_ref[:, ks]], axis=0)
        vb = jnp.concatenate([vp_ref[:, ks], vc_ref[:, ks]], axis=0)
        s = lax.dot_general(qh, kb, _TRANS_B, preferred_element_type=F32) * _SCALE
        s = jnp.where(vis, s, NEG_BIG)
        sink = jnp.zeros((rows, 1), F32)
        for g in range(GROUP):
            sink = jnp.where(rgrp == g, sink_ref[GROUP * h + g], sink)
        m = jnp.maximum(jnp.max(s, axis=-1, keepdims=True), sink)
        p = jnp.exp(s - m)
        denom = jnp.sum(p, axis=-1, keepdims=True) + jnp.exp(sink - m)
        o = jnp.dot(p.astype(BF16), vb, preferred_element_type=F32) / denom
        for g in range(GROUP):
            col = (GROUP * h + g) * HEAD_DIM
            gref, gcol = (glo_ref, col) if col < D_ATTN // 2 else (ghi_ref, col - D_ATTN // 2)
            gate = gref[:, gcol:gcol + HEAD_DIM].astype(F32)
            og = o[g * WINDOW:(g + 1) * WINDOW, :] * gate
            o_ref[:, col:col + HEAD_DIM] = og.astype(BF16)


def _attn_prompt(z, sinks):
    nb = SEQ // WINDOW
    half = D_ATTN // 2
    row = lambda b, i: b * nb + i
    prev = lambda b, i: b * nb + jnp.maximum(i - 1, 0)
    return pl.pallas_call(
        _attn_prompt_kernel,
        grid=(BATCH, nb),
        in_specs=[
            pl.BlockSpec((WINDOW, D_ATTN), lambda b, i: (row(b, i), COL_Q // D_ATTN)),
            pl.BlockSpec((WINDOW, D_KV), lambda b, i: (prev(b, i), COL_K // D_KV)),
            pl.BlockSpec((WINDOW, D_KV), lambda b, i: (row(b, i), COL_K // D_KV)),
            pl.BlockSpec((WINDOW, D_KV), lambda b, i: (prev(b, i), COL_V // D_KV)),
            pl.BlockSpec((WINDOW, D_KV), lambda b, i: (row(b, i), COL_V // D_KV)),
            pl.BlockSpec((WINDOW, half), lambda b, i: (row(b, i), COL_GATTN // half)),
            pl.BlockSpec((WINDOW, half), lambda b, i: (row(b, i), COL_GATTN // half + 1)),
            pl.BlockSpec(memory_space=pltpu.SMEM),
        ],
        out_specs=pl.BlockSpec((WINDOW, D_ATTN), lambda b, i: (row(b, i), 0)),
        out_shape=jax.ShapeDtypeStruct((M_PROMPT, D_ATTN), BF16),
        compiler_params=_cparams(("arbitrary", "arbitrary"), 32),
        name="attn_prompt",
    )(z, z, z, z, z, z, z, sinks)


def _attn_sample_kernel(q_ref, g_ref, ks_ref, vs_ref, ck_ref, cv_ref, sink_ref, o_ref, kw_ref, vw_ref):
    keep = (WINDOW - 1) * N_KV
    ones = jnp.ones((HEAD_DIM, LANES), BF16)

    def lane_sum(x):
        return jnp.dot(x.astype(BF16), ones, preferred_element_type=F32)

    def body(bi, carry):
        kc = ck_ref[bi, N_KV:, :]
        vc = cv_ref[bi, N_KV:, :]
        kn = ks_ref[bi]
        vn = vs_ref[bi]
        kc3 = kc.reshape(WINDOW - 1, N_KV, HEAD_DIM)
        vc3 = vc.reshape(WINDOW - 1, N_KV, HEAD_DIM)
        for g in range(GROUP):
            qg = q_ref[bi, g] * _SCALE
            s = lane_sum((kc3 * qg[None]).reshape(keep, HEAD_DIM)).reshape(WINDOW - 1, N_KV, LANES)
            sn = lane_sum(kn * qg)
            sink = sink_ref[g]
            m = jnp.maximum(jnp.maximum(jnp.max(s, axis=0), sn), sink)
            p = jnp.exp(s - m[None])
            pn = jnp.exp(sn - m)
            denom = jnp.sum(p, axis=0) + pn + jnp.exp(sink - m)
            o = (jnp.sum(p * vc3, axis=0) + pn * vn) / denom * g_ref[bi, g]
            for h in range(N_KV):
                o_ref[bi, pl.ds(GROUP * h + g, 1), :] = o[h:h + 1, :]
        kw_ref[bi, 0:keep, :] = kc
        kw_ref[bi, keep:, :] = kn
        vw_ref[bi, 0:keep, :] = vc
        vw_ref[bi, keep:, :] = vn
        return carry

    lax.fori_loop(0, BB_S, body, 0)


def _attn_sample(q_s, g_s, k_s, v_s, ck, cv, sinks_g):
    qb = pl.BlockSpec((BB_S, GROUP, N_KV, HEAD_DIM), lambda s: (s, 0, 0, 0))
    nb = pl.BlockSpec((BB_S, N_KV, HEAD_DIM), lambda s: (s, 0, 0))
    cb = pl.BlockSpec((BB_S, WINDOW * N_KV, HEAD_DIM), lambda s: (s, 0, 0))
    return pl.pallas_call(
        _attn_sample_kernel,
        grid=(DEC_BATCH // BB_S,),
        in_specs=[qb, qb, nb, nb, cb, cb, pl.BlockSpec((GROUP, N_KV, 1), lambda s: (0, 0, 0))],
        out_specs=[pl.BlockSpec((BB_S, N_HEADS, HEAD_DIM), lambda s: (s, 0, 0)), cb, cb],
        out_shape=[
            jax.ShapeDtypeStruct((DEC_BATCH, N_HEADS, HEAD_DIM), F32),
            jax.ShapeDtypeStruct((DEC_BATCH, WINDOW * N_KV, HEAD_DIM), F32),
            jax.ShapeDtypeStruct((DEC_BATCH, WINDOW * N_KV, HEAD_DIM), F32),
        ],
        compiler_params=_cparams(("arbitrary",), 48),
        name="attn_sample",
    )(q_s, g_s, k_s, v_s, ck, cv, sinks_g)


def _merge_kernel(y_ref, og_ref, wr_ref, wa_ref, mr_ref, ma_ref, o_ref):
    br = jnp.dot(y_ref[...], wr_ref[...], preferred_element_type=F32)
    ba = jnp.dot(og_ref[...], wa_ref[...], preferred_element_type=F32)
    o_ref[...] = (mr_ref[...].astype(F32) * br + ma_ref[...].astype(F32) * ba).astype(BF16)


def _merge(y, og, wr, wa, z, row0, tm, name):
    rows = y.shape[0]
    grid = (rows // tm, D_MODEL // TN_M)
    rb0 = row0 // tm
    act = pl.BlockSpec((tm, D_MODEL), lambda m, n: (m, 0))
    wsp = pl.BlockSpec((D_MODEL, TN_M), lambda m, n: (0, n))
    return pl.pallas_call(
        _merge_kernel,
        grid=grid,
        in_specs=[
            act, act, wsp, wsp,
            pl.BlockSpec((tm, TN_M), lambda m, n: (rb0 + m, COL_MRNN // TN_M + n)),
            pl.BlockSpec((tm, TN_M), lambda m, n: (rb0 + m, COL_MATTN // TN_M + n)),
        ],
        out_specs=pl.BlockSpec((tm, TN_M), lambda m, n: (m, n)),
        out_shape=jax.ShapeDtypeStruct((rows, D_MODEL), BF16),
        compiler_params=_cparams(("arbitrary", "arbitrary"), 56),
        name=name,
    )(y, og, wr, wa, z, z)


def _out_kernel(mp_ref, ms_ref, wo_ref, xp_ref, xs_ref, g_ref, b_ref, yp_ref, ys_ref):
    i = pl.program_id(0)
    m = jnp.where(i < M_PROMPT // TM_O, mp_ref[...], ms_ref[...])
    acc = jnp.dot(m, wo_ref[...], preferred_element_type=F32)

    def norm(x):
        r = DN_ALPHA * x + acc
        mu = jnp.mean(r, axis=-1, keepdims=True)
        d = r - mu
        var = jnp.mean(d * d, axis=-1, keepdims=True)
        return d * lax.rsqrt(var + LN_EPS) * g_ref[...] + b_ref[...]

    @pl.when(i < M_PROMPT // TM_O)
    def _():
        yp_ref[...] = norm(xp_ref[...])

    @pl.when(i == M_PROMPT // TM_O)
    def _():
        ys_ref[...] = norm(xs_ref[...])


def _out_proj(merged_p, merged_s, wo, xp, xs, ln_g, ln_b):
    last = M_PROMPT // TM_O - 1
    clamp = lambda i: (jnp.minimum(i, last), 0)
    const = lambda i: (0, 0)
    return pl.pallas_call(
        _out_kernel,
        grid=(M_ALL // TM_O,),
        in_specs=[
            pl.BlockSpec((TM_O, D_MODEL), clamp),
            pl.BlockSpec((DEC_BATCH, D_MODEL), const, pipeline_mode=pl.Buffered(1)),
            pl.BlockSpec((D_MODEL, D_MODEL), const, pipeline_mode=pl.Buffered(1)),
            pl.BlockSpec((TM_O, D_MODEL), clamp),
            pl.BlockSpec((DEC_BATCH, D_MODEL), const, pipeline_mode=pl.Buffered(1)),
            pl.BlockSpec((1, D_MODEL), const),
            pl.BlockSpec((1, D_MODEL), const),
        ],
        out_specs=[
            pl.BlockSpec((TM_O, D_MODEL), clamp),
            pl.BlockSpec((DEC_BATCH, D_MODEL), const),
        ],
        out_shape=[
            jax.ShapeDtypeStruct((M_PROMPT, D_MODEL), F32),
            jax.ShapeDtypeStruct((DEC_BATCH, D_MODEL), F32),
        ],
        compiler_params=_cparams(("arbitrary",), 56),
        name="out_proj",
    )(merged_p, merged_s, wo, xp, xs, ln_g, ln_b)


def _rope_tables():
    half = HEAD_DIM // 2
    pos = jnp.concatenate([
        jnp.tile(jnp.arange(SEQ, dtype=jnp.int32), BATCH),
        jnp.full((DEC_BATCH,), PAST_LEN, jnp.int32),
    ]).astype(F32)
    inv = ROPE_THETA ** (-jnp.arange(half, dtype=F32) / half)
    ang = pos[:, None] * inv[None, :]
    cos, sin = jnp.cos(ang), jnp.sin(ang)
    return jnp.concatenate([cos, cos], axis=-1), jnp.concatenate([-sin, sin], axis=-1)


def kernel(x_prompt, x_sample, state_conv, state_lru, cache_k_win, cache_v_win, w_in, conv_w, conv_b,
           w_gate_a, b_gate_a, w_gate_x, b_gate_x, lru_lambda, sinks, w_out_rnn, w_out_attn, w_o, ln_g, ln_b):
    xp = x_prompt.reshape(M_PROMPT, D_MODEL)
    xs = x_sample.reshape(DEC_BATCH, D_MODEL)
    x_bf = jnp.concatenate([xp.astype(BF16), xs.astype(BF16)], axis=0)
    cos_t, sin_t = _rope_tables()

    z = _inproj(x_bf, w_in[0], cos_t, sin_t)

    cw, cb = conv_w[0], conv_b
    wa, wx = w_gate_a[0].astype(BF16), w_gate_x[0].astype(BF16)
    ba, bx, lam = b_gate_a, b_gate_x, lru_lambda

    y_p, conv_p, lru_p, wr_bf, wat_bf, wo_bf = _rnn_prompt(
        z, cw, cb, wa, ba, wx, bx, lam, (w_out_rnn[0], w_out_attn[0], w_o[0]))
    zs = z[M_PROMPT:]
    u_s = zs[:, COL_U:COL_U + D_RNN].astype(F32)
    g_s = zs[:, COL_GRNN:COL_GRNN + D_RNN].astype(F32)
    sc_t = jnp.transpose(state_conv[0], (1, 0, 2))
    y_s, lru_s = _rnn_sample(u_s, g_s, sc_t, state_lru[0], cw, cb, wa, ba, wx, bx, lam)
    conv_s = jnp.stack([state_conv[0, :, 1], state_conv[0, :, 2], u_s], axis=1)

    og_p = _attn_prompt(z, sinks[0])

    def by_group(t):
        return t.astype(F32).reshape(DEC_BATCH, N_KV, GROUP, HEAD_DIM).transpose(0, 2, 1, 3)

    q_s = by_group(zs[:, COL_Q:COL_Q + D_ATTN])
    ga_s = by_group(zs[:, COL_GATTN:COL_GATTN + D_ATTN])
    k_s = zs[:, COL_K:COL_K + D_KV].astype(F32).reshape(DEC_BATCH, N_KV, HEAD_DIM)
    v_s = zs[:, COL_V:COL_V + D_KV].astype(F32).reshape(DEC_BATCH, N_KV, HEAD_DIM)
    ck = cache_k_win[0].reshape(DEC_BATCH, WINDOW * N_KV, HEAD_DIM)
    cv = cache_v_win[0].reshape(DEC_BATCH, WINDOW * N_KV, HEAD_DIM)
    sinks_g = sinks[0].reshape(N_KV, GROUP).T.reshape(GROUP, N_KV, 1)
    og_s, kw_s, vw_s = _attn_sample(q_s, ga_s, k_s, v_s, ck, cv, sinks_g)
    og_s = og_s.reshape(DEC_BATCH, D_ATTN).astype(BF16)

    merged_p = _merge(y_p, og_p, wr_bf, wat_bf, z, 0, TM_M, "merge_prompt")
    merged_s = _merge(y_s, og_s, wr_bf, wat_bf, z, M_PROMPT, DEC_BATCH, "merge_sample")
    out_p, out_s = _out_proj(merged_p, merged_s, wo_bf, xp, xs, ln_g, ln_b)

    kv_tail = jnp.stack([z[(b + 1) * SEQ - WINDOW:(b + 1) * SEQ, COL_K:COL_K + 2 * D_KV] for b in range(BATCH)])
    kv_tail = kv_tail.astype(F32)
    kw_p = kv_tail[:, :, :D_KV].reshape(BATCH, WINDOW, N_KV, HEAD_DIM)
    vw_p = kv_tail[:, :, D_KV:].reshape(BATCH, WINDOW, N_KV, HEAD_DIM)

    return (
        out_p.reshape(BATCH, SEQ, D_MODEL),
        out_s.reshape(DEC_BATCH, 1, D_MODEL),
        conv_p[None],
        lru_p.reshape(1, BATCH, D_RNN),
        kw_p[None],
        vw_p[None],
        conv_s[None],
        lru_s[None],
        kw_s.reshape(1, DEC_BATCH, WINDOW, N_KV, HEAD_DIM),
        vw_s.reshape(1, DEC_BATCH, WINDOW, N_KV, HEAD_DIM),
    )
```

```python
import functools

import jax
import jax.numpy as jnp
from jax import lax
from jax.experimental import pallas as pl
from jax.experimental.pallas import tpu as pltpu

F32 = jnp.float32
BF16 = jnp.bfloat16

D_MODEL = 4096
BATCH = 2
SEQ = 4096
DEC_BATCH = 128
PAST_LEN = 8192
D_RNN = D_MODEL
LRU_BLOCK = 256
LRU_BLOCKS = D_RNN // LRU_BLOCK
CONV_W = 4
LRU_C = 8.0
N_HEADS = 32
N_KV = 8
HEAD_DIM = 128
GROUP = N_HEADS // N_KV
D_ATTN = N_HEADS * HEAD_DIM
D_KV = N_KV * HEAD_DIM
WINDOW = 128
ROPE_THETA = 10000.0
LN_EPS = 1e-5
DN_ALPHA = 2.0 ** 0.25
NEG_BIG = -1e30
N_COLS = 2 * D_RNN + 2 * D_ATTN + 2 * D_KV + 2 * D_MODEL

M_PROMPT = BATCH * SEQ
M_ALL = M_PROMPT + DEC_BATCH

COL_U = 0
COL_GRNN = D_RNN
COL_Q = 2 * D_RNN
COL_K = COL_Q + D_ATTN
COL_V = COL_K + D_KV
COL_GATTN = COL_V + D_KV
COL_MRNN = COL_GATTN + D_ATTN
COL_MATTN = COL_MRNN + D_MODEL

LANES = 128
SUBLANES = 8
MIB = 1024 * 1024

TM_A = 832
TN_A = 1024
MT_A = M_ALL // TM_A
NT_A = N_COLS // TN_A
CK_A = 512
NCK_A = D_MODEL // CK_A
TT_R = 256
NJ_R = TT_R // SUBLANES
TC_R = 1024
RB_W = 128
NBLK_W = D_MODEL // RB_W
TM_M = 1024
TN_M = 256
TM_O = 128
BB_S = 8


def _cparams(sem, vmem_mib):
    return pltpu.CompilerParams(dimension_semantics=sem, vmem_limit_bytes=vmem_mib * MIB)


def _sigmoid(x):
    return 0.5 * jnp.tanh(0.5 * x) + 0.5


def _w_chunk_copy(w_hbm, stage, sem, n, c, slot):
    return pltpu.make_async_copy(
        w_hbm.at[pl.ds(c * CK_A, CK_A), pl.ds(n * TN_A, TN_A)], stage.at[slot], sem.at[slot])


def _inproj_kernel(x_ref, w_hbm, cos_ref, sin_ref, o_ref, acc_ref, wbf, stage, sem):
    s = pl.program_id(0)
    st = jnp.minimum(s, NT_A * MT_A - 1)
    n = st // MT_A
    m = st % MT_A
    has_next = (n + 1 < NT_A) & (s < NT_A * MT_A)
    cur = n % 2

    @pl.when(s == 0)
    def _():
        acc_ref[...] = jnp.zeros(acc_ref.shape, F32)
        for c in range(NCK_A):
            cp = _w_chunk_copy(w_hbm, stage, sem, 0, c, c % 2)
            cp.start()
            cp.wait()
            wbf[0, c] = stage[c % 2].astype(BF16)

    landed = (m >= 1) & (m <= NCK_A)

    @pl.when(has_next & landed)
    def _():
        _w_chunk_copy(w_hbm, stage, sem, n + 1, m - 1, (m - 1) % 2).wait()

    @pl.when(has_next & (m < NCK_A))
    def _():
        _w_chunk_copy(w_hbm, stage, sem, n + 1, m, m % 2).start()

    c = jnp.where(landed, m - 1, NCK_A - 1)
    wbf[1 - cur, c] = stage[c % 2].astype(BF16)

    pn = jnp.maximum(s - 1, 0) // MT_A

    def seg(lo, hi):
        return (pn >= lo // TN_A) & (pn < hi // TN_A)

    is_silu = seg(COL_GRNN, COL_Q) | seg(COL_GATTN, COL_MRNN)
    is_sig = seg(COL_MRNN, N_COLS)
    is_plain = seg(COL_U, COL_GRNN) | seg(COL_V, COL_GATTN)
    cos = cos_ref[...]
    sin = sin_ref[...]
    for h in range(TN_A // HEAD_DIM):
        sl = slice(h * HEAD_DIM, (h + 1) * HEAD_DIM)
        xh = acc_ref[:, sl]
        sg = _sigmoid(xh)
        rope = xh * cos + pltpu.roll(xh, HEAD_DIM // 2, axis=1) * sin
        out = jnp.where(is_plain, xh, jnp.where(is_silu, xh * sg, jnp.where(is_sig, sg, rope)))
        o_ref[:, sl] = out.astype(BF16)

    acc_ref[...] = jnp.dot(x_ref[...], wbf[cur].reshape(D_MODEL, TN_A), preferred_element_type=F32)


def _inproj(x_bf, w_f32, cos_t, sin_t):
    assert MT_A >= NCK_A + 1
    last = NT_A * MT_A - 1
    cur_rows = lambda s: (jnp.minimum(s, last) % MT_A, 0)
    prev_rows = lambda s: (jnp.maximum(s - 1, 0) % MT_A, 0)
    return pl.pallas_call(
        _inproj_kernel,
        grid=(NT_A * MT_A + 1,),
        in_specs=[
            pl.BlockSpec((TM_A, D_MODEL), cur_rows),
            pl.BlockSpec(memory_space=pl.ANY),
            pl.BlockSpec((TM_A, HEAD_DIM), prev_rows),
            pl.BlockSpec((TM_A, HEAD_DIM), prev_rows),
        ],
        out_specs=pl.BlockSpec(
            (TM_A, TN_A), lambda s: (jnp.maximum(s - 1, 0) % MT_A, jnp.maximum(s - 1, 0) // MT_A)),
        out_shape=jax.ShapeDtypeStruct((M_ALL, N_COLS), BF16),
        scratch_shapes=[
            pltpu.VMEM((TM_A, TN_A), F32),
            pltpu.VMEM((2, NCK_A, CK_A, TN_A), BF16),
            pltpu.VMEM((2, CK_A, TN_A), F32),
            pltpu.SemaphoreType.DMA((2,)),
        ],
        compiler_params=_cparams(("arbitrary",), 52),
        name="inproj",
    )(x_bf, w_f32, cos_t, sin_t)


def _softplus(x):
    return jnp.maximum(x, 0.0) + jnp.log1p(jnp.exp(-jnp.abs(x)))


def _lru_coeffs(xc, wa_ref, ba, wx_ref, bx, sp, blk):
    xb = xc.astype(BF16)
    r = _sigmoid(jnp.dot(xb, wa_ref[blk], preferred_element_type=F32) + ba)
    i = _sigmoid(jnp.dot(xb, wx_ref[blk], preferred_element_type=F32) + bx)
    a = jnp.exp(-LRU_C * r * sp)
    b = jnp.sqrt(1.0 - a * a) * i * xc
    return a, b


def _rnn_prompt_kernel(u_ref, g_ref, cw_ref, cb_ref, wa_ref, ba_ref, wx_ref, bx_ref, lam_ref,
                       p_ref, pt_ref, w0_ref, w1_ref, w2_ref,
                       y_ref, conv_ref, hl_ref, w0_out, w1_out, w2_out,
                       ubuf, a_s, b_s, tail, cin, hcar):
    t = pl.program_id(2)
    nv = CONV_W - 1

    step = (pl.program_id(0) * pl.num_programs(1) + pl.program_id(1)) * pl.num_programs(2) + t
    for k, (src, dst) in enumerate(((w0_ref, w0_out), (w1_ref, w1_out), (w2_ref, w2_out))):
        @pl.when((step >= k * NBLK_W) & (step < (k + 1) * NBLK_W))
        def _(src=src, dst=dst):
            dst[...] = src[...].astype(BF16)

    @pl.when(t == 0)
    def _():
        tail[...] = jnp.zeros(tail.shape, F32)
        hcar[...] = jnp.zeros(hcar.shape, F32)

    hist = nv * SUBLANES
    u_nat = u_ref[...]
    up = jnp.dot(p_ref[...], u_nat, preferred_element_type=F32)
    gp = jnp.dot(p_ref[...], g_ref[...], preferred_element_type=F32)
    ubuf[hist:hist + TT_R, :] = up

    sub0 = lax.broadcasted_iota(jnp.int32, (SUBLANES, TC_R), 0) == 0
    for i in range(nv):
        rs = slice(i * SUBLANES, (i + 1) * SUBLANES)
        cur = up[TT_R - hist + i * SUBLANES:TT_R - hist + (i + 1) * SUBLANES, :]
        ubuf[rs, :] = jnp.where(sub0, pltpu.roll(tail[rs, :], 1, axis=0), pltpu.roll(cur, 1, axis=0))
        tail[rs, :] = cur

    sp = _softplus(-lam_ref[...])
    for blk in range(TC_R // LRU_BLOCK):
        cs = slice(blk * LRU_BLOCK, (blk + 1) * LRU_BLOCK)
        xc = cb_ref[:, cs] + cw_ref[nv:nv + 1, cs] * ubuf[hist:hist + TT_R, cs]
        for j in range(nv):
            xc = xc + cw_ref[j:j + 1, cs] * ubuf[j * SUBLANES:j * SUBLANES + TT_R, cs]
        a, b = _lru_coeffs(xc, wa_ref, ba_ref[:, cs], wx_ref, bx_ref[:, cs], sp[:, cs], blk)
        a_s[:, cs] = a
        b_s[:, cs] = b

    h = jnp.zeros((SUBLANES, TC_R), F32)
    acc_a = jnp.ones((SUBLANES, TC_R), F32)
    for j in range(NJ_R):
        rs = slice(j * SUBLANES, (j + 1) * SUBLANES)
        aj = a_s[rs, :]
        h = aj * h + b_s[rs, :]
        acc_a = aj * acc_a
        b_s[rs, :] = h
        a_s[rs, :] = acc_a

    c = hcar[...]
    for s in range(SUBLANES):
        cin[s:s + 1, :] = c
        c = acc_a[s:s + 1, :] * c + h[s:s + 1, :]
    hcar[...] = c
    hl_ref[0] = c

    shape3 = (NJ_R, SUBLANES, TC_R)
    h_all = a_s[...].reshape(shape3) * cin[...][None] + b_s[...].reshape(shape3)
    yp = (h_all.reshape(TT_R, TC_R) * gp).astype(BF16)
    y_ref[...] = jnp.dot(pt_ref[...], yp, preferred_element_type=F32).astype(BF16)

    last = u_nat[TT_R - 2 * SUBLANES:, :].astype(F32)
    conv_ref[0] = last[2 * SUBLANES - nv:, :]


def _chunk_permutation():
    rho = jnp.arange(TT_R)
    src = (rho % SUBLANES) * NJ_R + rho // SUBLANES
    p = (src[:, None] == jnp.arange(TT_R)[None, :]).astype(BF16)
    return p, p.T


def _rnn_prompt(z, cw, cb, wa, ba, wx, bx, lam, w_side):
    nt = SEQ // TT_R
    nc = D_RNN // TC_R
    grid = (nc, BATCH, nt)
    assert len(w_side) * NBLK_W <= nc * BATCH * nt
    p, pt = _chunk_permutation()
    perm = pl.BlockSpec((TT_R, TT_R), lambda c, b, t: (0, 0))
    vec = lambda rows: pl.BlockSpec((rows, TC_R), lambda c, b, t: (0, c))
    gw = pl.BlockSpec((TC_R // LRU_BLOCK, LRU_BLOCK, LRU_BLOCK), lambda c, b, t: (c, 0, 0))

    def wblk(k):
        def imap(c, b, t):
            step = (c * BATCH + b) * nt + t
            return (jnp.clip(step - k * NBLK_W, 0, NBLK_W - 1), 0)
        return pl.BlockSpec((RB_W, D_MODEL), imap)

    return pl.pallas_call(
        _rnn_prompt_kernel,
        grid=grid,
        in_specs=[
            pl.BlockSpec((TT_R, TC_R), lambda c, b, t: (b * nt + t, COL_U // TC_R + c)),
            pl.BlockSpec((TT_R, TC_R), lambda c, b, t: (b * nt + t, COL_GRNN // TC_R + c)),
            vec(CONV_W), vec(1), gw, vec(1), gw, vec(1), vec(1),
            perm, perm,
            wblk(0), wblk(1), wblk(2),
        ],
        out_specs=[
            pl.BlockSpec((TT_R, TC_R), lambda c, b, t: (b * nt + t, c)),
            pl.BlockSpec((1, CONV_W - 1, TC_R), lambda c, b, t: (b, 0, c)),
            pl.BlockSpec((1, 1, TC_R), lambda c, b, t: (b, 0, c)),
            wblk(0), wblk(1), wblk(2),
        ],
        out_shape=[
            jax.ShapeDtypeStruct((M_PROMPT, D_RNN), BF16),
            jax.ShapeDtypeStruct((BATCH, CONV_W - 1, D_RNN), F32),
            jax.ShapeDtypeStruct((BATCH, 1, D_RNN), F32),
        ] + [jax.ShapeDtypeStruct((D_MODEL, D_MODEL), BF16)] * 3,
        scratch_shapes=[
            pltpu.VMEM((TT_R + (CONV_W - 1) * SUBLANES, TC_R), F32),
            pltpu.VMEM((TT_R, TC_R), F32),
            pltpu.VMEM((TT_R, TC_R), F32),
            pltpu.VMEM(((CONV_W - 1) * SUBLANES, TC_R), F32),
            pltpu.VMEM((SUBLANES, TC_R), F32),
            pltpu.VMEM((1, TC_R), F32),
        ],
        compiler_params=_cparams(("arbitrary", "arbitrary", "arbitrary"), 48),
        name="rnn_prompt",
    )(z, z, cw, cb, wa, ba, wx, bx, lam, p, pt, *w_side)


def _rnn_sample_kernel(u_ref, g_ref, sc_ref, h0_ref, cw_ref, cb_ref, wa_ref, ba_ref, wx_ref, bx_ref,
                       lam_ref, y_ref, h_ref):
    sp = _softplus(-lam_ref[...])
    for blk in range(TC_R // LRU_BLOCK):
        cs = slice(blk * LRU_BLOCK, (blk + 1) * LRU_BLOCK)
        xc = cb_ref[:, cs] + cw_ref[3:4, cs] * u_ref[:, cs]
        for j in range(CONV_W - 1):
            xc = xc + cw_ref[j:j + 1, cs] * sc_ref[j, :, cs]
        a, b = _lru_coeffs(xc, wa_ref, ba_ref[:, cs], wx_ref, bx_ref[:, cs], sp[:, cs], blk)
        h = a * h0_ref[:, cs] + b
        h_ref[:, cs] = h
        y_ref[:, cs] = (h * g_ref[:, cs]).astype(BF16)


def _rnn_sample(u_s, g_s, sc_t, h0, cw, cb, wa, ba, wx, bx, lam):
    nc = D_RNN // TC_R
    vec = lambda rows: pl.BlockSpec((rows, TC_R), lambda c: (0, c))
    gw = pl.BlockSpec((TC_R // LRU_BLOCK, LRU_BLOCK, LRU_BLOCK), lambda c: (c, 0, 0))
    return pl.pallas_call(
        _rnn_sample_kernel,
        grid=(nc,),
        in_specs=[
            vec(DEC_BATCH), vec(DEC_BATCH),
            pl.BlockSpec((CONV_W - 1, DEC_BATCH, TC_R), lambda c: (0, 0, c)),
            vec(DEC_BATCH), vec(CONV_W), vec(1), gw, vec(1), gw, vec(1), vec(1),
        ],
        out_specs=[vec(DEC_BATCH), vec(DEC_BATCH)],
        out_shape=[
            jax.ShapeDtypeStruct((DEC_BATCH, D_RNN), BF16),
            jax.ShapeDtypeStruct((DEC_BATCH, D_RNN), F32),
        ],
        compiler_params=_cparams(("arbitrary",), 32),
        name="rnn_sample",
    )(u_s, g_s, sc_t, h0, cw, cb, wa, ba, wx, bx, lam)


_SCALE = HEAD_DIM ** -0.5
_TRANS_B = (((1,), (1,)), ((), ()))


def _attn_prompt_kernel(q_ref, kp_ref, kc_ref, vp_ref, vc_ref, glo_ref, ghi_ref, sink_ref, o_ref):
    i = pl.program_id(1)
    rows = GROUP * WINDOW
    r = lax.broadcasted_iota(jnp.int32, (rows, 2 * WINDOW), 0) % WINDOW
    c = lax.broadcasted_iota(jnp.int32, (rows, 2 * WINDOW), 1)
    cmin = jnp.where(i > 0, 0, WINDOW)
    vis = (c > r) & (c <= r + WINDOW) & (c >= cmin)
    rgrp = lax.broadcasted_iota(jnp.int32, (rows, 1), 0) // WINDOW
    for h in range(N_KV):
        ks = slice(h * HEAD_DIM, (h + 1) * HEAD_DIM)
        qh = jnp.concatenate(
            [q_ref[:, (GROUP * h + g) * HEAD_DIM:(GROUP * h + g + 1) * HEAD_DIM] for g in range(GROUP)], axis=0)
        kb = jnp.concatenate([kp_ref[:, ks], kc_ref[:, ks]], axis=0)
        vb = jnp.concatenate([vp_ref[:, ks], vc_ref[:, ks]], axis=0)
        s = lax.dot_general(qh, kb, _TRANS_B, preferred_element_type=F32) * _SCALE
        s = jnp.where(vis, s, NEG_BIG)
        sink = jnp.zeros((rows, 1), F32)
        for g in range(GROUP):
            sink = jnp.where(rgrp == g, sink_ref[GROUP * h + g], sink)
        m = jnp.maximum(jnp.max(s, axis=-1, keepdims=True), sink)
        p = jnp.exp(s - m)
        denom = jnp.sum(p, axis=-1, keepdims=True) + jnp.exp(sink - m)
        o = jnp.dot(p.astype(BF16), vb, preferred_element_type=F32) / denom
        for g in range(GROUP):
            col = (GROUP * h + g) * HEAD_DIM
            gref, gcol = (glo_ref, col) if col < D_ATTN // 2 else (ghi_ref, col - D_ATTN // 2)
            gate = gref[:, gcol:gcol + HEAD_DIM].astype(F32)
            og = o[g * WINDOW:(g + 1) * WINDOW, :] * gate
            o_ref[:, col:col + HEAD_DIM] = og.astype(BF16)


def _attn_prompt(z, sinks):
    nb = SEQ // WINDOW
    half = D_ATTN // 2
    row = lambda b, i: b * nb + i
    prev = lambda b, i: b * nb + jnp.maximum(i - 1, 0)
    return pl.pallas_call(
        _attn_prompt_kernel,
        grid=(BATCH, nb),
        in_specs=[
            pl.BlockSpec((WINDOW, D_ATTN), lambda b, i: (row(b, i), COL_Q // D_ATTN)),
            pl.BlockSpec((WINDOW, D_KV), lambda b, i: (prev(b, i), COL_K // D_KV)),
            pl.BlockSpec((WINDOW, D_KV), lambda b, i: (row(b, i), COL_K // D_KV)),
            pl.BlockSpec((WINDOW, D_KV), lambda b, i: (prev(b, i), COL_V // D_KV)),
            pl.BlockSpec((WINDOW, D_KV), lambda b, i: (row(b, i), COL_V // D_KV)),
            pl.BlockSpec((WINDOW, half), lambda b, i: (row(b, i), COL_GATTN // half)),
            pl.BlockSpec((WINDOW, half), lambda b, i: (row(b, i), COL_GATTN // half + 1)),
            pl.BlockSpec(memory_space=pltpu.SMEM),
        ],
        out_specs=pl.BlockSpec((WINDOW, D_ATTN), lambda b, i: (row(b, i), 0)),
        out_shape=jax.ShapeDtypeStruct((M_PROMPT, D_ATTN), BF16),
        compiler_params=_cparams(("arbitrary", "arbitrary"), 32),
        name="attn_prompt",
    )(z, z, z, z, z, z, z, sinks)


def _attn_sample_kernel(q_ref, g_ref, ks_ref, vs_ref, ck_ref, cv_ref, sink_ref, o_ref, kw_ref, vw_ref):
    keep = (WINDOW - 1) * N_KV
    ones = jnp.ones((HEAD_DIM, LANES), BF16)

    def lane_sum(x):
        return jnp.dot(x.astype(BF16), ones, preferred_element_type=F32)

    def body(bi, carry):
        kc = ck_ref[bi, N_KV:, :]
        vc = cv_ref[bi, N_KV:, :]
        kn = ks_ref[bi]
        vn = vs_ref[bi]
        kc3 = kc.reshape(WINDOW - 1, N_KV, HEAD_DIM)
        vc3 = vc.reshape(WINDOW - 1, N_KV, HEAD_DIM)
        for g in range(GROUP):
            qg = q_ref[bi, g] * _SCALE
            s = lane_sum((kc3 * qg[None]).reshape(keep, HEAD_DIM)).reshape(WINDOW - 1, N_KV, LANES)
            sn = lane_sum(kn * qg)
            sink = sink_ref[g]
            m = jnp.maximum(jnp.maximum(jnp.max(s, axis=0), sn), sink)
            p = jnp.exp(s - m[None])
            pn = jnp.exp(sn - m)
            denom = jnp.sum(p, axis=0) + pn + jnp.exp(sink - m)
            o = (jnp.sum(p * vc3, axis=0) + pn * vn) / denom * g_ref[bi, g]
            for h in range(N_KV):
                o_ref[bi, pl.ds(GROUP * h + g, 1), :] = o[h:h + 1, :]
        kw_ref[bi, 0:keep, :] = kc
        kw_ref[bi, keep:, :] = kn
        vw_ref[bi, 0:keep, :] = vc
        vw_ref[bi, keep:, :] = vn
        return carry

    lax.fori_loop(0, BB_S, body, 0)


def _attn_sample(q_s, g_s, k_s, v_s, ck, cv, sinks_g):
    qb = pl.BlockSpec((BB_S, GROUP, N_KV, HEAD_DIM), lambda s: (s, 0, 0, 0))
    nb = pl.BlockSpec((BB_S, N_KV, HEAD_DIM), lambda s: (s, 0, 0))
    cb = pl.BlockSpec((BB_S, WINDOW * N_KV, HEAD_DIM), lambda s: (s, 0, 0))
    return pl.pallas_call(
        _attn_sample_kernel,
        grid=(DEC_BATCH // BB_S,),
        in_specs=[qb, qb, nb, nb, cb, cb, pl.BlockSpec((GROUP, N_KV, 1), lambda s: (0, 0, 0))],
        out_specs=[pl.BlockSpec((BB_S, N_HEADS, HEAD_DIM), lambda s: (s, 0, 0)), cb, cb],
        out_shape=[
            jax.ShapeDtypeStruct((DEC_BATCH, N_HEADS, HEAD_DIM), F32),
            jax.ShapeDtypeStruct((DEC_BATCH, WINDOW * N_KV, HEAD_DIM), F32),
            jax.ShapeDtypeStruct((DEC_BATCH, WINDOW * N_KV, HEAD_DIM), F32),
        ],
        compiler_params=_cparams(("arbitrary",), 48),
        name="attn_sample",
    )(q_s, g_s, k_s, v_s, ck, cv, sinks_g)


def _merge_kernel(y_ref, og_ref, wr_ref, wa_ref, mr_ref, ma_ref, o_ref):
    br = jnp.dot(y_ref[...], wr_ref[...], preferred_element_type=F32)
    ba = jnp.dot(og_ref[...], wa_ref[...], preferred_element_type=F32)
    o_ref[...] = (mr_ref[...].astype(F32) * br + ma_ref[...].astype(F32) * ba).astype(BF16)


def _merge(y, og, wr, wa, z, row0, tm, name):
    rows = y.shape[0]
    grid = (rows // tm, D_MODEL // TN_M)
    rb0 = row0 // tm
    act = pl.BlockSpec((tm, D_MODEL), lambda m, n: (m, 0))
    wsp = pl.BlockSpec((D_MODEL, TN_M), lambda m, n: (0, n))
    return pl.pallas_call(
        _merge_kernel,
        grid=grid,
        in_specs=[
            act, act, wsp, wsp,
            pl.BlockSpec((tm, TN_M), lambda m, n: (rb0 + m, COL_MRNN // TN_M + n)),
            pl.BlockSpec((tm, TN_M), lambda m, n: (rb0 + m, COL_MATTN // TN_M + n)),
        ],
        out_specs=pl.BlockSpec((tm, TN_M), lambda m, n: (m, n)),
        out_shape=jax.ShapeDtypeStruct((rows, D_MODEL), BF16),
        compiler_params=_cparams(("arbitrary", "arbitrary"), 56),
        name=name,
    )(y, og, wr, wa, z, z)


def _out_kernel(mp_ref, ms_ref, wo_ref, xp_ref, xs_ref, g_ref, b_ref, yp_ref, ys_ref):
    is_prompt = pl.program_id(0) < M_PROMPT // TM_O
    m = jnp.where(is_prompt, mp_ref[...], ms_ref[...])
    x = jnp.where(is_prompt, xp_ref[...], xs_ref[...])
    r = DN_ALPHA * x + jnp.dot(m, wo_ref[...], preferred_element_type=F32)
    mu = jnp.mean(r, axis=-1, keepdims=True)
    d = r - mu
    var = jnp.mean(d * d, axis=-1, keepdims=True)
    res = d * lax.rsqrt(var + LN_EPS) * g_ref[...] + b_ref[...]

    @pl.when(is_prompt)
    def _():
        yp_ref[...] = res

    @pl.when(jnp.logical_not(is_prompt))
    def _():
        ys_ref[...] = res


def _out_proj(merged_p, merged_s, wo, xp, xs, ln_g, ln_b):
    last = M_PROMPT // TM_O - 1
    clamp = lambda i: (jnp.minimum(i, last), 0)
    const = lambda i: (0, 0)
    return pl.pallas_call(
        _out_kernel,
        grid=(M_ALL // TM_O,),
        in_specs=[
            pl.BlockSpec((TM_O, D_MODEL), clamp),
            pl.BlockSpec((DEC_BATCH, D_MODEL), const, pipeline_mode=pl.Buffered(1)),
            pl.BlockSpec((D_MODEL, D_MODEL), const, pipeline_mode=pl.Buffered(1)),
            pl.BlockSpec((TM_O, D_MODEL), clamp),
            pl.BlockSpec((DEC_BATCH, D_MODEL), const, pipeline_mode=pl.Buffered(1)),
            pl.BlockSpec((1, D_MODEL), const),
            pl.BlockSpec((1, D_MODEL), const),
        ],
        out_specs=[
            pl.BlockSpec((TM_O, D_MODEL), clamp),
            pl.BlockSpec((DEC_BATCH, D_MODEL), const),
        ],
        out_shape=[
            jax.ShapeDtypeStruct((M_PROMPT, D_MODEL), F32),
            jax.ShapeDtypeStruct((DEC_BATCH, D_MODEL), F32),
        ],
        compiler_params=_cparams(("arbitrary",), 56),
        name="out_proj",
    )(merged_p, merged_s, wo, xp, xs, ln_g, ln_b)


def _rope_tables():
    half = HEAD_DIM // 2
    pos = jnp.concatenate([
        jnp.tile(jnp.arange(SEQ, dtype=jnp.int32), BATCH),
        jnp.full((DEC_BATCH,), PAST_LEN, jnp.int32),
    ]).astype(F32)
    inv = ROPE_THETA ** (-jnp.arange(half, dtype=F32) / half)
    ang = pos[:, None] * inv[None, :]
    cos, sin = jnp.cos(ang), jnp.sin(ang)
    return jnp.concatenate([cos, cos], axis=-1), jnp.concatenate([-sin, sin], axis=-1)


def kernel(x_prompt, x_sample, state_conv, state_lru, cache_k_win, cache_v_win, w_in, conv_w, conv_b,
           w_gate_a, b_gate_a, w_gate_x, b_gate_x, lru_lambda, sinks, w_out_rnn, w_out_attn, w_o, ln_g, ln_b):
    xp = x_prompt.reshape(M_PROMPT, D_MODEL)
    xs = x_sample.reshape(DEC_BATCH, D_MODEL)
    x_bf = jnp.concatenate([xp.astype(BF16), xs.astype(BF16)], axis=0)
    cos_t, sin_t = _rope_tables()

    z = _inproj(x_bf, w_in[0], cos_t, sin_t)

    cw, cb = conv_w[0], conv_b
    wa, wx = w_gate_a[0].astype(BF16), w_gate_x[0].astype(BF16)
    ba, bx, lam = b_gate_a, b_gate_x, lru_lambda

    y_p, conv_p, lru_p, wr_bf, wat_bf, wo_bf = _rnn_prompt(
        z, cw, cb, wa, ba, wx, bx, lam, (w_out_rnn[0], w_out_attn[0], w_o[0]))
    zs = z[M_PROMPT:]
    u_s = zs[:, COL_U:COL_U + D_RNN].astype(F32)
    g_s = zs[:, COL_GRNN:COL_GRNN + D_RNN].astype(F32)
    sc_t = jnp.transpose(state_conv[0], (1, 0, 2))
    y_s, lru_s = _rnn_sample(u_s, g_s, sc_t, state_lru[0], cw, cb, wa, ba, wx, bx, lam)
    conv_s = jnp.stack([state_conv[0, :, 1], state_conv[0, :, 2], u_s], axis=1)

    og_p = _attn_prompt(z, sinks[0])

    def by_group(t):
        return t.astype(F32).reshape(DEC_BATCH, N_KV, GROUP, HEAD_DIM).transpose(0, 2, 1, 3)

    q_s = by_group(zs[:, COL_Q:COL_Q + D_ATTN])
    ga_s = by_group(zs[:, COL_GATTN:COL_GATTN + D_ATTN])
    k_s = zs[:, COL_K:COL_K + D_KV].astype(F32).reshape(DEC_BATCH, N_KV, HEAD_DIM)
    v_s = zs[:, COL_V:COL_V + D_KV].astype(F32).reshape(DEC_BATCH, N_KV, HEAD_DIM)
    ck = cache_k_win[0].reshape(DEC_BATCH, WINDOW * N_KV, HEAD_DIM)
    cv = cache_v_win[0].reshape(DEC_BATCH, WINDOW * N_KV, HEAD_DIM)
    sinks_g = sinks[0].reshape(N_KV, GROUP).T.reshape(GROUP, N_KV, 1)
    og_s, kw_s, vw_s = _attn_sample(q_s, ga_s, k_s, v_s, ck, cv, sinks_g)
    og_s = og_s.reshape(DEC_BATCH, D_ATTN).astype(BF16)

    merged_p = _merge(y_p, og_p, wr_bf, wat_bf, z, 0, TM_M, "merge_prompt")
    merged_s = _merge(y_s, og_s, wr_bf, wat_bf, z, M_PROMPT, DEC_BATCH, "merge_sample")
    out_p, out_s = _out_proj(merged_p, merged_s, wo_bf, xp, xs, ln_g, ln_b)

    kv_tail = jnp.stack([z[(b + 1) * SEQ - WINDOW:(b + 1) * SEQ, COL_K:COL_K + 2 * D_KV] for b in range(BATCH)])
    kv_tail = kv_tail.astype(F32)
    kw_p = kv_tail[:, :, :D_KV].reshape(BATCH, WINDOW, N_KV, HEAD_DIM)
    vw_p = kv_tail[:, :, D_KV:].reshape(BATCH, WINDOW, N_KV, HEAD_DIM)

    return (
        out_p.reshape(BATCH, SEQ, D_MODEL),
        out_s.reshape(DEC_BATCH, 1, D_MODEL),
        conv_p[None],
        lru_p.reshape(1, BATCH, D_RNN),
        kw_p[None],
        vw_p[None],
        conv_s[None],
        lru_s[None],
        kw_s.reshape(1, DEC_BATCH, WINDOW, N_KV, HEAD_DIM),
        vw_s.reshape(1, DEC_BATCH, WINDOW, N_KV, HEAD_DIM),
    )
```

```python
import functools

import jax
import jax.numpy as jnp
from jax import lax
from jax.experimental import pallas as pl
from jax.experimental.pallas import tpu as pltpu

F32 = jnp.float32
BF16 = jnp.bfloat16

D_MODEL = 4096
BATCH = 2
SEQ = 4096
DEC_BATCH = 128
PAST_LEN = 8192
D_RNN = D_MODEL
LRU_BLOCK = 256
LRU_BLOCKS = D_RNN // LRU_BLOCK
CONV_W = 4
LRU_C = 8.0
N_HEADS = 32
N_KV = 8
HEAD_DIM = 128
GROUP = N_HEADS // N_KV
D_ATTN = N_HEADS * HEAD_DIM
D_KV = N_KV * HEAD_DIM
WINDOW = 128
ROPE_THETA = 10000.0
LN_EPS = 1e-5
DN_ALPHA = 2.0 ** 0.25
NEG_BIG = -1e30
N_COLS = 2 * D_RNN + 2 * D_ATTN + 2 * D_KV + 2 * D_MODEL

M_PROMPT = BATCH * SEQ
M_ALL = M_PROMPT + DEC_BATCH

COL_U = 0
COL_GRNN = D_RNN
COL_Q = 2 * D_RNN
COL_K = COL_Q + D_ATTN
COL_V = COL_K + D_KV
COL_GATTN = COL_V + D_KV
COL_MRNN = COL_GATTN + D_ATTN
COL_MATTN = COL_MRNN + D_MODEL

LANES = 128
SUBLANES = 8
MIB = 1024 * 1024

TM_A = 832
TN_A = 1024
MT_A = M_ALL // TM_A
NT_A = N_COLS // TN_A
CK_A = 512
NCK_A = D_MODEL // CK_A
TT_R = 256
NJ_R = TT_R // SUBLANES
TC_R = 1024
RB_W = 128
NBLK_W = D_MODEL // RB_W
TM_M = 1024
TN_M = 256
TM_O = 256
RG_O = 128
TM_X = DEC_BATCH


def _cparams(sem, vmem_mib):
    return pltpu.CompilerParams(dimension_semantics=sem, vmem_limit_bytes=vmem_mib * MIB)


def _cast_kernel(xp_ref, xs_ref, o_ref):
    is_prompt = pl.program_id(0) < M_PROMPT // TM_X
    o_ref[...] = jnp.where(is_prompt, xp_ref[...], xs_ref[...]).astype(BF16)


def _cast_rows(xp, xs):
    last = M_PROMPT // TM_X - 1
    return pl.pallas_call(
        _cast_kernel,
        grid=(M_ALL // TM_X,),
        in_specs=[
            pl.BlockSpec((TM_X, D_MODEL), lambda i: (jnp.minimum(i, last), 0)),
            pl.BlockSpec((DEC_BATCH, D_MODEL), lambda i: (0, 0)),
        ],
        out_specs=pl.BlockSpec((TM_X, D_MODEL), lambda i: (i, 0)),
        out_shape=jax.ShapeDtypeStruct((M_ALL, D_MODEL), BF16),
        compiler_params=_cparams(("arbitrary",), 32),
        name="cast_rows",
    )(xp, xs)


def _sigmoid(x):
    return 0.5 * jnp.tanh(0.5 * x) + 0.5


def _w_chunk_copy(w_hbm, stage, sem, n, c, slot):
    return pltpu.make_async_copy(
        w_hbm.at[pl.ds(c * CK_A, CK_A), pl.ds(n * TN_A, TN_A)], stage.at[slot], sem.at[slot])


def _inproj_kernel(x_ref, w_hbm, cos_ref, sin_ref, o_ref, acc_ref, wbf, stage, sem):
    s = pl.program_id(0)
    st = jnp.minimum(s, NT_A * MT_A - 1)
    n = st // MT_A
    m = st % MT_A
    has_next = (n + 1 < NT_A) & (s < NT_A * MT_A)
    cur = n % 2

    @pl.when(s == 0)
    def _():
        acc_ref[...] = jnp.zeros(acc_ref.shape, F32)
        for c in range(NCK_A):
            cp = _w_chunk_copy(w_hbm, stage, sem, 0, c, c % 2)
            cp.start()
            cp.wait()
            wbf[0, c] = stage[c % 2].astype(BF16)

    landed = (m >= 1) & (m <= NCK_A)

    @pl.when(has_next & landed)
    def _():
        _w_chunk_copy(w_hbm, stage, sem, n + 1, m - 1, (m - 1) % 2).wait()

    @pl.when(has_next & (m < NCK_A))
    def _():
        _w_chunk_copy(w_hbm, stage, sem, n + 1, m, m % 2).start()

    c = jnp.where(landed, m - 1, NCK_A - 1)
    wbf[1 - cur, c] = stage[c % 2].astype(BF16)

    pn = jnp.maximum(s - 1, 0) // MT_A

    def seg(lo, hi):
        return (pn >= lo // TN_A) & (pn < hi // TN_A)

    is_silu = seg(COL_GRNN, COL_Q) | seg(COL_GATTN, COL_MRNN)
    is_sig = seg(COL_MRNN, N_COLS)
    is_plain = seg(COL_U, COL_GRNN) | seg(COL_V, COL_GATTN)
    cos = cos_ref[...]
    sin = sin_ref[...]
    for h in range(TN_A // HEAD_DIM):
        sl = slice(h * HEAD_DIM, (h + 1) * HEAD_DIM)
        xh = acc_ref[:, sl]
        sg = _sigmoid(xh)
        rope = xh * cos + pltpu.roll(xh, HEAD_DIM // 2, axis=1) * sin
        out = jnp.where(is_plain, xh, jnp.where(is_silu, xh * sg, jnp.where(is_sig, sg, rope)))
        o_ref[:, sl] = out.astype(BF16)

    acc_ref[...] = jnp.dot(x_ref[...], wbf[cur].reshape(D_MODEL, TN_A), preferred_element_type=F32)


def _inproj(x_bf, w_f32, cos_t, sin_t):
    assert MT_A >= NCK_A + 1
    last = NT_A * MT_A - 1
    cur_rows = lambda s: (jnp.minimum(s, last) % MT_A, 0)
    prev_rows = lambda s: (jnp.maximum(s - 1, 0) % MT_A, 0)
    return pl.pallas_call(
        _inproj_kernel,
        grid=(NT_A * MT_A + 1,),
        in_specs=[
            pl.BlockSpec((TM_A, D_MODEL), cur_rows),
            pl.BlockSpec(memory_space=pl.ANY),
            pl.BlockSpec((TM_A, HEAD_DIM), prev_rows),
            pl.BlockSpec((TM_A, HEAD_DIM), prev_rows),
        ],
        out_specs=pl.BlockSpec(
            (TM_A, TN_A), lambda s: (jnp.maximum(s - 1, 0) % MT_A, jnp.maximum(s - 1, 0) // MT_A)),
        out_shape=jax.ShapeDtypeStruct((M_ALL, N_COLS), BF16),
        scratch_shapes=[
            pltpu.VMEM((TM_A, TN_A), F32),
            pltpu.VMEM((2, NCK_A, CK_A, TN_A), BF16),
            pltpu.VMEM((2, CK_A, TN_A), F32),
            pltpu.SemaphoreType.DMA((2,)),
        ],
        compiler_params=_cparams(("arbitrary",), 52),
        name="inproj",
    )(x_bf, w_f32, cos_t, sin_t)


def _softplus(x):
    return jnp.maximum(x, 0.0) + jnp.log1p(jnp.exp(-jnp.abs(x)))


def _lru_coeffs(xc, wa_ref, ba, wx_ref, bx, sp, blk):
    xb = xc.astype(BF16)
    r = _sigmoid(jnp.dot(xb, wa_ref[blk], preferred_element_type=F32) + ba)
    i = _sigmoid(jnp.dot(xb, wx_ref[blk], preferred_element_type=F32) + bx)
    a = jnp.exp(-LRU_C * r * sp)
    b = jnp.sqrt(1.0 - a * a) * i * xc
    return a, b


def _rnn_prompt_kernel(u_ref, g_ref, cw_ref, cb_ref, wa_ref, ba_ref, wx_ref, bx_ref, lam_ref,
                       p_ref, pt_ref, w0_ref, w1_ref, w2_ref,
                       y_ref, conv_ref, hl_ref, w0_out, w1_out, w2_out,
                       ubuf, a_s, b_s, tail, cin, hcar):
    t = pl.program_id(2)
    nv = CONV_W - 1

    step = (pl.program_id(0) * pl.num_programs(1) + pl.program_id(1)) * pl.num_programs(2) + t
    for k, (src, dst) in enumerate(((w0_ref, w0_out), (w1_ref, w1_out), (w2_ref, w2_out))):
        @pl.when((step >= k * NBLK_W) & (step < (k + 1) * NBLK_W))
        def _(src=src, dst=dst):
            dst[...] = src[...].astype(BF16)

    @pl.when(t == 0)
    def _():
        tail[...] = jnp.zeros(tail.shape, F32)
        hcar[...] = jnp.zeros(hcar.shape, F32)

    hist = nv * SUBLANES
    u_nat = u_ref[...]
    up = jnp.dot(p_ref[...], u_nat, preferred_element_type=F32)
    gp = jnp.dot(p_ref[...], g_ref[...], preferred_element_type=F32)
    ubuf[hist:hist + TT_R, :] = up

    sub0 = lax.broadcasted_iota(jnp.int32, (SUBLANES, TC_R), 0) == 0
    for i in range(nv):
        rs = slice(i * SUBLANES, (i + 1) * SUBLANES)
        cur = up[TT_R - hist + i * SUBLANES:TT_R - hist + (i + 1) * SUBLANES, :]
        ubuf[rs, :] = jnp.where(sub0, pltpu.roll(tail[rs, :], 1, axis=0), pltpu.roll(cur, 1, axis=0))
        tail[rs, :] = cur

    sp = _softplus(-lam_ref[...])
    for blk in range(TC_R // LRU_BLOCK):
        cs = slice(blk * LRU_BLOCK, (blk + 1) * LRU_BLOCK)
        xc = cb_ref[:, cs] + cw_ref[nv:nv + 1, cs] * ubuf[hist:hist + TT_R, cs]
        for j in range(nv):
            xc = xc + cw_ref[j:j + 1, cs] * ubuf[j * SUBLANES:j * SUBLANES + TT_R, cs]
        a, b = _lru_coeffs(xc, wa_ref, ba_ref[:, cs], wx_ref, bx_ref[:, cs], sp[:, cs], blk)
        a_s[:, cs] = a
        b_s[:, cs] = b

    h = jnp.zeros((SUBLANES, TC_R), F32)
    acc_a = jnp.ones((SUBLANES, TC_R), F32)
    for j in range(NJ_R):
        rs = slice(j * SUBLANES, (j + 1) * SUBLANES)
        aj = a_s[rs, :]
        h = aj * h + b_s[rs, :]
        acc_a = aj * acc_a
        b_s[rs, :] = h
        a_s[rs, :] = acc_a

    c = hcar[...]
    for s in range(SUBLANES):
        cin[s:s + 1, :] = c
        c = acc_a[s:s + 1, :] * c + h[s:s + 1, :]
    hcar[...] = c
    hl_ref[0] = c

    shape3 = (NJ_R, SUBLANES, TC_R)
    h_all = a_s[...].reshape(shape3) * cin[...][None] + b_s[...].reshape(shape3)
    yp = (h_all.reshape(TT_R, TC_R) * gp).astype(BF16)
    y_ref[...] = jnp.dot(pt_ref[...], yp, preferred_element_type=F32).astype(BF16)

    last = u_nat[TT_R - 2 * SUBLANES:, :].astype(F32)
    conv_ref[0] = last[2 * SUBLANES - nv:, :]


def _chunk_permutation():
    rho = jnp.arange(TT_R)
    src = (rho % SUBLANES) * NJ_R + rho // SUBLANES
    p = (src[:, None] == jnp.arange(TT_R)[None, :]).astype(BF16)
    return p, p.T


def _rnn_prompt(z, cw, cb, wa, ba, wx, bx, lam, w_side):
    nt = SEQ // TT_R
    nc = D_RNN // TC_R
    grid = (nc, BATCH, nt)
    assert len(w_side) * NBLK_W <= nc * BATCH * nt
    p, pt = _chunk_permutation()
    perm = pl.BlockSpec((TT_R, TT_R), lambda c, b, t: (0, 0))
    vec = lambda rows: pl.BlockSpec((rows, TC_R), lambda c, b, t: (0, c))
    gw = pl.BlockSpec((TC_R // LRU_BLOCK, LRU_BLOCK, LRU_BLOCK), lambda c, b, t: (c, 0, 0))

    def wblk(k):
        def imap(c, b, t):
            step = (c * BATCH + b) * nt + t
            return (jnp.clip(step - k * NBLK_W, 0, NBLK_W - 1), 0)
        return pl.BlockSpec((RB_W, D_MODEL), imap)

    return pl.pallas_call(
        _rnn_prompt_kernel,
        grid=grid,
        in_specs=[
            pl.BlockSpec((TT_R, TC_R), lambda c, b, t: (b * nt + t, COL_U // TC_R + c)),
            pl.BlockSpec((TT_R, TC_R), lambda c, b, t: (b * nt + t, COL_GRNN // TC_R + c)),
            vec(CONV_W), vec(1), gw, vec(1), gw, vec(1), vec(1),
            perm, perm,
            wblk(0), wblk(1), wblk(2),
        ],
        out_specs=[
            pl.BlockSpec((TT_R, TC_R), lambda c, b, t: (b * nt + t, c)),
            pl.BlockSpec((1, CONV_W - 1, TC_R), lambda c, b, t: (b, 0, c)),
            pl.BlockSpec((1, 1, TC_R), lambda c, b, t: (b, 0, c)),
            wblk(0), wblk(1), wblk(2),
        ],
        out_shape=[
            jax.ShapeDtypeStruct((M_PROMPT, D_RNN), BF16),
            jax.ShapeDtypeStruct((BATCH, CONV_W - 1, D_RNN), F32),
            jax.ShapeDtypeStruct((BATCH, 1, D_RNN), F32),
        ] + [jax.ShapeDtypeStruct((D_MODEL, D_MODEL), BF16)] * 3,
        scratch_shapes=[
            pltpu.VMEM((TT_R + (CONV_W - 1) * SUBLANES, TC_R), F32),
            pltpu.VMEM((TT_R, TC_R), F32),
            pltpu.VMEM((TT_R, TC_R), F32),
            pltpu.VMEM(((CONV_W - 1) * SUBLANES, TC_R), F32),
            pltpu.VMEM((SUBLANES, TC_R), F32),
            pltpu.VMEM((1, TC_R), F32),
        ],
        compiler_params=_cparams(("arbitrary", "arbitrary", "arbitrary"), 48),
        name="rnn_prompt",
    )(z, z, cw, cb, wa, ba, wx, bx, lam, p, pt, *w_side)


def _rnn_sample_kernel(u_ref, g_ref, sc_ref, h0_ref, cw_ref, cb_ref, wa_ref, ba_ref, wx_ref, bx_ref,
                       lam_ref, y_ref, h_ref):
    sp = _softplus(-lam_ref[...])
    for blk in range(TC_R // LRU_BLOCK):
        cs = slice(blk * LRU_BLOCK, (blk + 1) * LRU_BLOCK)
        xc = cb_ref[:, cs] + cw_ref[3:4, cs] * u_ref[:, cs]
        for j in range(CONV_W - 1):
            xc = xc + cw_ref[j:j + 1, cs] * sc_ref[j, :, cs]
        a, b = _lru_coeffs(xc, wa_ref, ba_ref[:, cs], wx_ref, bx_ref[:, cs], sp[:, cs], blk)
        h = a * h0_ref[:, cs] + b
        h_ref[:, cs] = h
        y_ref[:, cs] = (h * g_ref[:, cs]).astype(BF16)


def _rnn_sample(u_s, g_s, sc_t, h0, cw, cb, wa, ba, wx, bx, lam):
    nc = D_RNN // TC_R
    vec = lambda rows: pl.BlockSpec((rows, TC_R), lambda c: (0, c))
    gw = pl.BlockSpec((TC_R // LRU_BLOCK, LRU_BLOCK, LRU_BLOCK), lambda c: (c, 0, 0))
    return pl.pallas_call(
        _rnn_sample_kernel,
        grid=(nc,),
        in_specs=[
            vec(DEC_BATCH), vec(DEC_BATCH),
            pl.BlockSpec((CONV_W - 1, DEC_BATCH, TC_R), lambda c: (0, 0, c)),
            vec(DEC_BATCH), vec(CONV_W), vec(1), gw, vec(1), gw, vec(1), vec(1),
        ],
        out_specs=[vec(DEC_BATCH), vec(DEC_BATCH)],
        out_shape=[
            jax.ShapeDtypeStruct((DEC_BATCH, D_RNN), BF16),
            jax.ShapeDtypeStruct((DEC_BATCH, D_RNN), F32),
        ],
        compiler_params=_cparams(("arbitrary",), 32),
        name="rnn_sample",
    )(u_s, g_s, sc_t, h0, cw, cb, wa, ba, wx, bx, lam)


_SCALE = HEAD_DIM ** -0.5
_TRANS_B = (((1,), (1,)), ((), ()))


def _attn_prompt_kernel(q_ref, kp_ref, kc_ref, vp_ref, vc_ref, glo_ref, ghi_ref, sink_ref, o_ref):
    i = pl.program_id(1)
    rows = GROUP * WINDOW
    r = lax.broadcasted_iota(jnp.int32, (rows, 2 * WINDOW), 0) % WINDOW
    c = lax.broadcasted_iota(jnp.int32, (rows, 2 * WINDOW), 1)
    cmin = jnp.where(i > 0, 0, WINDOW)
    vis = (c > r) & (c <= r + WINDOW) & (c >= cmin)
    rgrp = lax.broadcasted_iota(jnp.int32, (rows, 1), 0) // WINDOW
    for h in range(N_KV):
        ks = slice(h * HEAD_DIM, (h + 1) * HEAD_DIM)
        qh = jnp.concatenate(
            [q_ref[:, (GROUP * h + g) * HEAD_DIM:(GROUP * h + g + 1) * HEAD_DIM] for g in range(GROUP)], axis=0)
        kb = jnp.concatenate([kp_ref[:, ks], kc_ref[:, ks]], axis=0)
        vb = jnp.concatenate([vp_ref[:, ks], vc_ref[:, ks]], axis=0)
        s = lax.dot_general(qh, kb, _TRANS_B, preferred_element_type=F32) * _SCALE
        s = jnp.where(vis, s, NEG_BIG)
        sink = jnp.zeros((rows, 1), F32)
        for g in range(GROUP):
            sink = jnp.where(rgrp == g, sink_ref[GROUP * h + g], sink)
        m = jnp.maximum(jnp.max(s, axis=-1, keepdims=True), sink)
        p = jnp.exp(s - m)
        denom = jnp.sum(p, axis=-1, keepdims=True) + jnp.exp(sink - m)
        o = jnp.dot(p.astype(BF16), vb, preferred_element_type=F32) / denom
        for g in range(GROUP):
            col = (GROUP * h + g) * HEAD_DIM
            gref, gcol = (glo_ref, col) if col < D_ATTN // 2 else (ghi_ref, col - D_ATTN // 2)
            gate = gref[:, gcol:gcol + HEAD_DIM].astype(F32)
            og = o[g * WINDOW:(g + 1) * WINDOW, :] * gate
            o_ref[:, col:col + HEAD_DIM] = og.astype(BF16)


def _attn_prompt(z, sinks):
    nb = SEQ // WINDOW
    half = D_ATTN // 2
    row = lambda b, i: b * nb + i
    prev = lambda b, i: b * nb + jnp.maximum(i - 1, 0)
    return pl.pallas_call(
        _attn_prompt_kernel,
        grid=(BATCH, nb),
        in_specs=[
            pl.BlockSpec((WINDOW, D_ATTN), lambda b, i: (row(b, i), COL_Q // D_ATTN)),
            pl.BlockSpec((WINDOW, D_KV), lambda b, i: (prev(b, i), COL_K // D_KV)),
            pl.BlockSpec((WINDOW, D_KV), lambda b, i: (row(b, i), COL_K // D_KV)),
            pl.BlockSpec((WINDOW, D_KV), lambda b, i: (prev(b, i), COL_V // D_KV)),
            pl.BlockSpec((WINDOW, D_KV), lambda b, i: (row(b, i), COL_V // D_KV)),
            pl.BlockSpec((WINDOW, half), lambda b, i: (row(b, i), COL_GATTN // half)),
            pl.BlockSpec((WINDOW, half), lambda b, i: (row(b, i), COL_GATTN // half + 1)),
            pl.BlockSpec(memory_space=pltpu.SMEM),
        ],
        out_specs=pl.BlockSpec((WINDOW, D_ATTN), lambda b, i: (row(b, i), 0)),
        out_shape=jax.ShapeDtypeStruct((M_PROMPT, D_ATTN), BF16),
        compiler_params=_cparams(("arbitrary", "arbitrary"), 32),
        name="attn_prompt",
    )(z, z, z, z, z, z, z, sinks)


def _attn_sample_one(q_ref, g_ref, ks_ref, vs_ref, ck_ref, cv_ref, sink_ref, o_ref, kw_ref, vw_ref):
    keep = (WINDOW - 1) * N_KV
    kc = ck_ref[0, N_KV:, :]
    vc = cv_ref[0, N_KV:, :]
    kn = ks_ref[0]
    vn = vs_ref[0]
    kc3 = kc.reshape(WINDOW - 1, N_KV, HEAD_DIM)
    vc3 = vc.reshape(WINDOW - 1, N_KV, HEAD_DIM)
    for g in range(GROUP):
        qg = q_ref[0, g] * _SCALE
        s = jnp.sum(kc3 * qg[None], axis=-1, keepdims=True)
        sn = jnp.sum(kn * qg, axis=-1, keepdims=True)
        sink = sink_ref[g]
        m = jnp.maximum(jnp.maximum(jnp.max(s, axis=0), sn), sink)
        p = jnp.exp(s - m[None])
        pn = jnp.exp(sn - m)
        denom = jnp.sum(p, axis=0) + pn + jnp.exp(sink - m)
        o = (jnp.sum(p * vc3, axis=0) + pn * vn) / denom * g_ref[0, g]
        for h in range(N_KV):
            o_ref[0, pl.ds(GROUP * h + g, 1), :] = o[h:h + 1, :]
    kw_ref[0, 0:keep, :] = kc
    kw_ref[0, keep:, :] = kn
    vw_ref[0, 0:keep, :] = vc
    vw_ref[0, keep:, :] = vn


def _merge_kernel(y_ref, og_ref, wr_ref, wa_ref, mr_ref, ma_ref, o_ref):
    br = jnp.dot(y_ref[...], wr_ref[...], preferred_element_type=F32)
    ba = jnp.dot(og_ref[...], wa_ref[...], preferred_element_type=F32)
    o_ref[...] = (mr_ref[...].astype(F32) * br + ma_ref[...].astype(F32) * ba).astype(BF16)


def _merge_prompt_kernel(y_ref, og_ref, wr_ref, wa_ref, mr_ref, ma_ref,
                         q_ref, g_ref, ks_ref, vs_ref, ck_ref, cv_ref, sink_ref,
                         o_ref, os_ref, kw_ref, vw_ref):
    _attn_sample_one(q_ref, g_ref, ks_ref, vs_ref, ck_ref, cv_ref, sink_ref, os_ref, kw_ref, vw_ref)
    _merge_kernel(y_ref, og_ref, wr_ref, wa_ref, mr_ref, ma_ref, o_ref)


def _merge_specs(tm, rb0):
    act = pl.BlockSpec((tm, D_MODEL), lambda m, n: (m, 0))
    wsp = pl.BlockSpec((D_MODEL, TN_M), lambda m, n: (0, n))
    in_specs = [
        act, act, wsp, wsp,
        pl.BlockSpec((tm, TN_M), lambda m, n: (rb0 + m, COL_MRNN // TN_M + n)),
        pl.BlockSpec((tm, TN_M), lambda m, n: (rb0 + m, COL_MATTN // TN_M + n)),
    ]
    return in_specs, pl.BlockSpec((tm, TN_M), lambda m, n: (m, n))


def _merge_sample(y, og, wr, wa, z):
    in_specs, out_spec = _merge_specs(DEC_BATCH, M_PROMPT // DEC_BATCH)
    return pl.pallas_call(
        _merge_kernel,
        grid=(1, D_MODEL // TN_M),
        in_specs=in_specs,
        out_specs=out_spec,
        out_shape=jax.ShapeDtypeStruct((DEC_BATCH, D_MODEL), BF16),
        compiler_params=_cparams(("arbitrary", "arbitrary"), 32),
        name="merge_sample",
    )(y, og, wr, wa, z, z)


def _merge_prompt(y, og, wr, wa, z, q_s, g_s, k_s, v_s, ck, cv, sinks_g):
    nm, nn = M_PROMPT // TM_M, D_MODEL // TN_M
    assert nm * nn == DEC_BATCH
    in_specs, out_spec = _merge_specs(TM_M, 0)
    qb = pl.BlockSpec((1, GROUP, N_KV, HEAD_DIM), lambda m, n: (m * nn + n, 0, 0, 0))
    nb = pl.BlockSpec((1, N_KV, HEAD_DIM), lambda m, n: (m * nn + n, 0, 0))
    cb = pl.BlockSpec((1, WINDOW * N_KV, HEAD_DIM), lambda m, n: (m * nn + n, 0, 0))
    return pl.pallas_call(
        _merge_prompt_kernel,
        grid=(nm, nn),
        in_specs=in_specs + [qb, qb, nb, nb, cb, cb, pl.BlockSpec((GROUP, N_KV, 1), lambda m, n: (0, 0, 0))],
        out_specs=[out_spec, pl.BlockSpec((1, N_HEADS, HEAD_DIM), lambda m, n: (m * nn + n, 0, 0)), cb, cb],
        out_shape=[
            jax.ShapeDtypeStruct((M_PROMPT, D_MODEL), BF16),
            jax.ShapeDtypeStruct((DEC_BATCH, N_HEADS, HEAD_DIM), F32),
            jax.ShapeDtypeStruct((DEC_BATCH, WINDOW * N_KV, HEAD_DIM), F32),
            jax.ShapeDtypeStruct((DEC_BATCH, WINDOW * N_KV, HEAD_DIM), F32),
        ],
        compiler_params=_cparams(("arbitrary", "arbitrary"), 56),
        name="merge_prompt",
    )(y, og, wr, wa, z, z, q_s, g_s, k_s, v_s, ck, cv, sinks_g)


def _out_kernel(m_ref, wo_ref, x_ref, g_ref, b_ref, y_ref):
    y_ref[...] = DN_ALPHA * x_ref[...] + jnp.dot(m_ref[...], wo_ref[...], preferred_element_type=F32)
    for i in range(y_ref.shape[0] // RG_O):
        rs = slice(i * RG_O, (i + 1) * RG_O)
        r = y_ref[rs, :]
        mu = jnp.mean(r, axis=-1, keepdims=True)
        d = r - mu
        var = jnp.mean(d * d, axis=-1, keepdims=True)
        y_ref[rs, :] = d * lax.rsqrt(var + LN_EPS) * g_ref[...] + b_ref[...]


def _out_proj(merged, wo, x, ln_g, ln_b, tm, name):
    rows = merged.shape[0]
    row = pl.BlockSpec((tm, D_MODEL), lambda i: (i, 0))
    const = lambda i: (0, 0)
    return pl.pallas_call(
        _out_kernel,
        grid=(rows // tm,),
        in_specs=[
            row,
            pl.BlockSpec((D_MODEL, D_MODEL), const, pipeline_mode=pl.Buffered(1)),
            row,
            pl.BlockSpec((1, D_MODEL), const),
            pl.BlockSpec((1, D_MODEL), const),
        ],
        out_specs=row,
        out_shape=jax.ShapeDtypeStruct((rows, D_MODEL), F32),
        compiler_params=_cparams(("arbitrary",), 60),
        name=name,
    )(merged, wo, x, ln_g, ln_b)


def _rope_tables():
    half = HEAD_DIM // 2
    pos = jnp.concatenate([
        jnp.tile(jnp.arange(SEQ, dtype=jnp.int32), BATCH),
        jnp.full((DEC_BATCH,), PAST_LEN, jnp.int32),
    ]).astype(F32)
    inv = ROPE_THETA ** (-jnp.arange(half, dtype=F32) / half)
    ang = pos[:, None] * inv[None, :]
    cos, sin = jnp.cos(ang), jnp.sin(ang)
    return jnp.concatenate([cos, cos], axis=-1), jnp.concatenate([-sin, sin], axis=-1)


def kernel(x_prompt, x_sample, state_conv, state_lru, cache_k_win, cache_v_win, w_in, conv_w, conv_b,
           w_gate_a, b_gate_a, w_gate_x, b_gate_x, lru_lambda, sinks, w_out_rnn, w_out_attn, w_o, ln_g, ln_b):
    xp = x_prompt.reshape(M_PROMPT, D_MODEL)
    xs = x_sample.reshape(DEC_BATCH, D_MODEL)
    x_bf = _cast_rows(xp, xs)
    cos_t, sin_t = _rope_tables()

    z = _inproj(x_bf, w_in[0], cos_t, sin_t)

    cw, cb = conv_w[0], conv_b
    wa, wx = w_gate_a[0].astype(BF16), w_gate_x[0].astype(BF16)
    ba, bx, lam = b_gate_a, b_gate_x, lru_lambda

    y_p, conv_p, lru_p, wr_bf, wat_bf, wo_bf = _rnn_prompt(
        z, cw, cb, wa, ba, wx, bx, lam, (w_out_rnn[0], w_out_attn[0], w_o[0]))
    zs = z[M_PROMPT:]
    u_s = zs[:, COL_U:COL_U + D_RNN].astype(F32)
    g_s = zs[:, COL_GRNN:COL_GRNN + D_RNN].astype(F32)
    sc_t = jnp.transpose(state_conv[0], (1, 0, 2))
    y_s, lru_s = _rnn_sample(u_s, g_s, sc_t, state_lru[0], cw, cb, wa, ba, wx, bx, lam)
    conv_s = jnp.stack([state_conv[0, :, 1], state_conv[0, :, 2], u_s], axis=1)

    og_p = _attn_prompt(z, sinks[0])

    def by_group(t):
        return t.astype(F32).reshape(DEC_BATCH, N_KV, GROUP, HEAD_DIM).transpose(0, 2, 1, 3)

    q_s = by_group(zs[:, COL_Q:COL_Q + D_ATTN])
    ga_s = by_group(zs[:, COL_GATTN:COL_GATTN + D_ATTN])
    k_s = zs[:, COL_K:COL_K + D_KV].astype(F32).reshape(DEC_BATCH, N_KV, HEAD_DIM)
    v_s = zs[:, COL_V:COL_V + D_KV].astype(F32).reshape(DEC_BATCH, N_KV, HEAD_DIM)
    ck = cache_k_win[0].reshape(DEC_BATCH, WINDOW * N_KV, HEAD_DIM)
    cv = cache_v_win[0].reshape(DEC_BATCH, WINDOW * N_KV, HEAD_DIM)
    sinks_g = sinks[0].reshape(N_KV, GROUP).T.reshape(GROUP, N_KV, 1)

    merged_p, og_s, kw_s, vw_s = _merge_prompt(
        y_p, og_p, wr_bf, wat_bf, z, q_s, ga_s, k_s, v_s, ck, cv, sinks_g)
    og_s = og_s.reshape(DEC_BATCH, D_ATTN).astype(BF16)
    merged_s = _merge_sample(y_s, og_s, wr_bf, wat_bf, z)
    out_p = _out_proj(merged_p, wo_bf, xp, ln_g, ln_b, TM_O, "out_proj_prompt")
    out_s = _out_proj(merged_s, wo_bf, xs, ln_g, ln_b, DEC_BATCH, "out_proj_sample")

    kv_tail = jnp.stack([z[(b + 1) * SEQ - WINDOW:(b + 1) * SEQ, COL_K:COL_K + 2 * D_KV] for b in range(BATCH)])
    kv_tail = kv_tail.astype(F32)
    kw_p = kv_tail[:, :, :D_KV].reshape(BATCH, WINDOW, N_KV, HEAD_DIM)
    vw_p = kv_tail[:, :, D_KV:].reshape(BATCH, WINDOW, N_KV, HEAD_DIM)

    return (
        out_p.reshape(BATCH, SEQ, D_MODEL),
        out_s.reshape(DEC_BATCH, 1, D_MODEL),
        conv_p[None],
        lru_p.reshape(1, BATCH, D_RNN),
        kw_p[None],
        vw_p[None],
        conv_s[None],
        lru_s[None],
        kw_s.reshape(1, DEC_BATCH, WINDOW, N_KV, HEAD_DIM),
        vw_s.reshape(1, DEC_BATCH, WINDOW, N_KV, HEAD_DIM),
    )
```

```python
import functools

import jax
import jax.numpy as jnp
from jax import lax
from jax.experimental import pallas as pl
from jax.experimental.pallas import tpu as pltpu

F32 = jnp.float32
BF16 = jnp.bfloat16

D_MODEL = 4096
BATCH = 2
SEQ = 4096
DEC_BATCH = 128
PAST_LEN = 8192
D_RNN = D_MODEL
LRU_BLOCK = 256
LRU_BLOCKS = D_RNN // LRU_BLOCK
CONV_W = 4
LRU_C = 8.0
N_HEADS = 32
N_KV = 8
HEAD_DIM = 128
GROUP = N_HEADS // N_KV
D_ATTN = N_HEADS * HEAD_DIM
D_KV = N_KV * HEAD_DIM
WINDOW = 128
ROPE_THETA = 10000.0
LN_EPS = 1e-5
DN_ALPHA = 2.0 ** 0.25
NEG_BIG = -1e30
N_COLS = 2 * D_RNN + 2 * D_ATTN + 2 * D_KV + 2 * D_MODEL

M_PROMPT = BATCH * SEQ
M_ALL = M_PROMPT + DEC_BATCH

COL_U = 0
COL_GRNN = D_RNN
COL_Q = 2 * D_RNN
COL_K = COL_Q + D_ATTN
COL_V = COL_K + D_KV
COL_GATTN = COL_V + D_KV
COL_MRNN = COL_GATTN + D_ATTN
COL_MATTN = COL_MRNN + D_MODEL

LANES = 128
SUBLANES = 8
MIB = 1024 * 1024

TM_A = 832
TN_A = 1024
MT_A = M_ALL // TM_A
NT_A = N_COLS // TN_A
CK_A = 512
NCK_A = D_MODEL // CK_A
TT_R = 256
NJ_R = TT_R // SUBLANES
TC_R = 1024
RB_W = 128
NBLK_W = D_MODEL // RB_W
TM_M = 1024
TN_M = 256
TM_O = 256
RG_O = 128
TM_X = DEC_BATCH


def _cparams(sem, vmem_mib):
    return pltpu.CompilerParams(dimension_semantics=sem, vmem_limit_bytes=vmem_mib * MIB)


def _cast_kernel(xp_ref, xs_ref, o_ref):
    is_prompt = pl.program_id(0) < M_PROMPT // TM_X
    o_ref[...] = jnp.where(is_prompt, xp_ref[...], xs_ref[...]).astype(BF16)


def _cast_rows(xp, xs):
    last = M_PROMPT // TM_X - 1
    return pl.pallas_call(
        _cast_kernel,
        grid=(M_ALL // TM_X,),
        in_specs=[
            pl.BlockSpec((TM_X, D_MODEL), lambda i: (jnp.minimum(i, last), 0)),
            pl.BlockSpec((DEC_BATCH, D_MODEL), lambda i: (0, 0)),
        ],
        out_specs=pl.BlockSpec((TM_X, D_MODEL), lambda i: (i, 0)),
        out_shape=jax.ShapeDtypeStruct((M_ALL, D_MODEL), BF16),
        compiler_params=_cparams(("arbitrary",), 32),
        name="cast_rows",
    )(xp, xs)


def _sigmoid(x):
    return 0.5 * jnp.tanh(0.5 * x) + 0.5


def _w_chunk_copy(w_hbm, stage, sem, n, c, slot):
    return pltpu.make_async_copy(
        w_hbm.at[pl.ds(c * CK_A, CK_A), pl.ds(n * TN_A, TN_A)], stage.at[slot], sem.at[slot])


def _inproj_kernel(n0, nt, side, n_side_in, n_side_out, x_ref, w_hbm, cos_ref, sin_ref, *rest):
    side_in = rest[:n_side_in]
    o_ref = rest[n_side_in]
    side_out = rest[n_side_in + 1:n_side_in + 1 + n_side_out]
    acc_ref, wbf, stage, sem = rest[n_side_in + 1 + n_side_out:n_side_in + 5 + n_side_out]
    side_scratch = rest[n_side_in + 5 + n_side_out:]

    s = pl.program_id(0)
    st = jnp.minimum(s, nt * MT_A - 1)
    nl = st // MT_A
    m = st % MT_A
    has_next = (nl + 1 < nt) & (s < nt * MT_A)
    cur = nl % 2

    @pl.when(s == 0)
    def _():
        acc_ref[...] = jnp.zeros(acc_ref.shape, F32)
        for c in range(NCK_A):
            cp = _w_chunk_copy(w_hbm, stage, sem, n0, c, c % 2)
            cp.start()
            cp.wait()
            wbf[0, c] = stage[c % 2].astype(BF16)

    landed = (m >= 1) & (m <= NCK_A)

    @pl.when(has_next & landed)
    def _():
        _w_chunk_copy(w_hbm, stage, sem, n0 + nl + 1, m - 1, (m - 1) % 2).wait()

    @pl.when(has_next & (m < NCK_A))
    def _():
        _w_chunk_copy(w_hbm, stage, sem, n0 + nl + 1, m, m % 2).start()

    c = jnp.where(landed, m - 1, NCK_A - 1)
    wbf[1 - cur, c] = stage[c % 2].astype(BF16)

    if side is not None:
        side(s, side_in, side_out, side_scratch)

    pn = n0 + jnp.maximum(s - 1, 0) // MT_A

    def seg(lo, hi):
        return (pn >= lo // TN_A) & (pn < hi // TN_A)

    is_silu = seg(COL_GRNN, COL_Q) | seg(COL_GATTN, COL_MRNN)
    is_sig = seg(COL_MRNN, N_COLS)
    is_plain = seg(COL_U, COL_GRNN) | seg(COL_V, COL_GATTN)
    has_rope = n0 * TN_A < COL_V and (n0 + nt) * TN_A > COL_Q
    cos = cos_ref[...]
    sin = sin_ref[...]
    for h in range(TN_A // HEAD_DIM):
        sl = slice(h * HEAD_DIM, (h + 1) * HEAD_DIM)
        xh = acc_ref[:, sl]
        sg = _sigmoid(xh)
        out = jnp.where(is_silu, xh * sg, sg)
        if has_rope:
            rope = xh * cos + pltpu.roll(xh, HEAD_DIM // 2, axis=1) * sin
            out = jnp.where(is_sig | is_silu, out, rope)
        o_ref[:, sl] = jnp.where(is_plain, xh, out).astype(BF16)

    acc_ref[...] = jnp.dot(x_ref[...], wbf[cur].reshape(D_MODEL, TN_A), preferred_element_type=F32)


def _inproj(x_bf, w_f32, cos_t, sin_t, col0, ncols, name, side=None, side_args=(), side_in_specs=(),
            side_out_specs=(), side_out_shape=(), side_scratch=(), vmem_mib=52):
    assert MT_A >= NCK_A + 1
    n0, nt = col0 // TN_A, ncols // TN_A
    last = nt * MT_A - 1
    cur_rows = lambda s: (jnp.minimum(s, last) % MT_A, 0)
    prev_rows = lambda s: (jnp.maximum(s - 1, 0) % MT_A, 0)
    kern = functools.partial(_inproj_kernel, n0, nt, side, len(side_in_specs), len(side_out_specs))
    return pl.pallas_call(
        kern,
        grid=(nt * MT_A + 1,),
        in_specs=[
            pl.BlockSpec((TM_A, D_MODEL), cur_rows),
            pl.BlockSpec(memory_space=pl.ANY),
            pl.BlockSpec((TM_A, HEAD_DIM), prev_rows),
            pl.BlockSpec((TM_A, HEAD_DIM), prev_rows),
            *side_in_specs,
        ],
        out_specs=[
            pl.BlockSpec(
                (TM_A, TN_A), lambda s: (jnp.maximum(s - 1, 0) % MT_A, jnp.maximum(s - 1, 0) // MT_A)),
            *side_out_specs,
        ],
        out_shape=[jax.ShapeDtypeStruct((M_ALL, ncols), BF16), *side_out_shape],
        scratch_shapes=[
            pltpu.VMEM((TM_A, TN_A), F32),
            pltpu.VMEM((2, NCK_A, CK_A, TN_A), BF16),
            pltpu.VMEM((2, CK_A, TN_A), F32),
            pltpu.SemaphoreType.DMA((2,)),
            *side_scratch,
        ],
        compiler_params=_cparams(("arbitrary",), vmem_mib),
        name=name,
    )(x_bf, w_f32, cos_t, sin_t, *side_args)


def _softplus(x):
    return jnp.maximum(x, 0.0) + jnp.log1p(jnp.exp(-jnp.abs(x)))


def _lru_coeffs(xc, wa_ref, ba, wx_ref, bx, sp, blk):
    xb = xc.astype(BF16)
    r = _sigmoid(jnp.dot(xb, wa_ref[blk], preferred_element_type=F32) + ba)
    i = _sigmoid(jnp.dot(xb, wx_ref[blk], preferred_element_type=F32) + bx)
    a = jnp.exp(-LRU_C * r * sp)
    b = jnp.sqrt(1.0 - a * a) * i * xc
    return a, b


def _rnn_prompt_kernel(u_ref, g_ref, cw_ref, cb_ref, wa_ref, ba_ref, wx_ref, bx_ref, lam_ref,
                       p_ref, pt_ref, w0_ref, w1_ref, w2_ref,
                       y_ref, conv_ref, hl_ref, w0_out, w1_out, w2_out,
                       ubuf, a_s, b_s, tail, cin, hcar):
    t = pl.program_id(2)
    nv = CONV_W - 1

    step = (pl.program_id(0) * pl.num_programs(1) + pl.program_id(1)) * pl.num_programs(2) + t
    for k, (src, dst) in enumerate(((w0_ref, w0_out), (w1_ref, w1_out), (w2_ref, w2_out))):
        @pl.when((step >= k * NBLK_W) & (step < (k + 1) * NBLK_W))
        def _(src=src, dst=dst):
            dst[...] = src[...].astype(BF16)

    @pl.when(t == 0)
    def _():
        tail[...] = jnp.zeros(tail.shape, F32)
        hcar[...] = jnp.zeros(hcar.shape, F32)

    hist = nv * SUBLANES
    u_nat = u_ref[...]
    up = jnp.dot(p_ref[...], u_nat, preferred_element_type=F32)
    gp = jnp.dot(p_ref[...], g_ref[...], preferred_element_type=F32)
    ubuf[hist:hist + TT_R, :] = up

    sub0 = lax.broadcasted_iota(jnp.int32, (SUBLANES, TC_R), 0) == 0
    for i in range(nv):
        rs = slice(i * SUBLANES, (i + 1) * SUBLANES)
        cur = up[TT_R - hist + i * SUBLANES:TT_R - hist + (i + 1) * SUBLANES, :]
        ubuf[rs, :] = jnp.where(sub0, pltpu.roll(tail[rs, :], 1, axis=0), pltpu.roll(cur, 1, axis=0))
        tail[rs, :] = cur

    sp = _softplus(-lam_ref[...])
    for blk in range(TC_R // LRU_BLOCK):
        cs = slice(blk * LRU_BLOCK, (blk + 1) * LRU_BLOCK)
        xc = cb_ref[:, cs] + cw_ref[nv:nv + 1, cs] * ubuf[hist:hist + TT_R, cs]
        for j in range(nv):
            xc = xc + cw_ref[j:j + 1, cs] * ubuf[j * SUBLANES:j * SUBLANES + TT_R, cs]
        a, b = _lru_coeffs(xc, wa_ref, ba_ref[:, cs], wx_ref, bx_ref[:, cs], sp[:, cs], blk)
        a_s[:, cs] = a
        b_s[:, cs] = b

    h = jnp.zeros((SUBLANES, TC_R), F32)
    acc_a = jnp.ones((SUBLANES, TC_R), F32)
    for j in range(NJ_R):
        rs = slice(j * SUBLANES, (j + 1) * SUBLANES)
        aj = a_s[rs, :]
        h = aj * h + b_s[rs, :]
        acc_a = aj * acc_a
        b_s[rs, :] = h
        a_s[rs, :] = acc_a

    c = hcar[...]
    for s in range(SUBLANES):
        cin[s:s + 1, :] = c
        c = acc_a[s:s + 1, :] * c + h[s:s + 1, :]
    hcar[...] = c
    hl_ref[0] = c

    shape3 = (NJ_R, SUBLANES, TC_R)
    h_all = a_s[...].reshape(shape3) * cin[...][None] + b_s[...].reshape(shape3)
    yp = (h_all.reshape(TT_R, TC_R) * gp).astype(BF16)
    y_ref[...] = jnp.dot(pt_ref[...], yp, preferred_element_type=F32).astype(BF16)

    last = u_nat[TT_R - 2 * SUBLANES:, :].astype(F32)
    conv_ref[0] = last[2 * SUBLANES - nv:, :]


def _chunk_permutation():
    rho = jnp.arange(TT_R)
    src = (rho % SUBLANES) * NJ_R + rho // SUBLANES
    p = (src[:, None] == jnp.arange(TT_R)[None, :]).astype(BF16)
    return p, p.T


def _rnn_prompt(z, cw, cb, wa, ba, wx, bx, lam, w_side):
    nt = SEQ // TT_R
    nc = D_RNN // TC_R
    grid = (nc, BATCH, nt)
    assert len(w_side) * NBLK_W <= nc * BATCH * nt
    p, pt = _chunk_permutation()
    perm = pl.BlockSpec((TT_R, TT_R), lambda c, b, t: (0, 0))
    vec = lambda rows: pl.BlockSpec((rows, TC_R), lambda c, b, t: (0, c))
    gw = pl.BlockSpec((TC_R // LRU_BLOCK, LRU_BLOCK, LRU_BLOCK), lambda c, b, t: (c, 0, 0))

    def wblk(k):
        def imap(c, b, t):
            step = (c * BATCH + b) * nt + t
            return (jnp.clip(step - k * NBLK_W, 0, NBLK_W - 1), 0)
        return pl.BlockSpec((RB_W, D_MODEL), imap)

    return pl.pallas_call(
        _rnn_prompt_kernel,
        grid=grid,
        in_specs=[
            pl.BlockSpec((TT_R, TC_R), lambda c, b, t: (b * nt + t, COL_U // TC_R + c)),
            pl.BlockSpec((TT_R, TC_R), lambda c, b, t: (b * nt + t, COL_GRNN // TC_R + c)),
            vec(CONV_W), vec(1), gw, vec(1), gw, vec(1), vec(1),
            perm, perm,
            wblk(0), wblk(1), wblk(2),
        ],
        out_specs=[
            pl.BlockSpec((TT_R, TC_R), lambda c, b, t: (b * nt + t, c)),
            pl.BlockSpec((1, CONV_W - 1, TC_R), lambda c, b, t: (b, 0, c)),
            pl.BlockSpec((1, 1, TC_R), lambda c, b, t: (b, 0, c)),
            wblk(0), wblk(1), wblk(2),
        ],
        out_shape=[
            jax.ShapeDtypeStruct((M_PROMPT, D_RNN), BF16),
            jax.ShapeDtypeStruct((BATCH, CONV_W - 1, D_RNN), F32),
            jax.ShapeDtypeStruct((BATCH, 1, D_RNN), F32),
        ] + [jax.ShapeDtypeStruct((D_MODEL, D_MODEL), BF16)] * 3,
        scratch_shapes=[
            pltpu.VMEM((TT_R + (CONV_W - 1) * SUBLANES, TC_R), F32),
            pltpu.VMEM((TT_R, TC_R), F32),
            pltpu.VMEM((TT_R, TC_R), F32),
            pltpu.VMEM(((CONV_W - 1) * SUBLANES, TC_R), F32),
            pltpu.VMEM((SUBLANES, TC_R), F32),
            pltpu.VMEM((1, TC_R), F32),
        ],
        compiler_params=_cparams(("arbitrary", "arbitrary", "arbitrary"), 48),
        name="rnn_prompt",
    )(z, z, cw, cb, wa, ba, wx, bx, lam, p, pt, *w_side)


def _rnn_sample_kernel(u_ref, g_ref, sc_ref, h0_ref, cw_ref, cb_ref, wa_ref, ba_ref, wx_ref, bx_ref,
                       lam_ref, y_ref, h_ref):
    sp = _softplus(-lam_ref[...])
    for blk in range(TC_R // LRU_BLOCK):
        cs = slice(blk * LRU_BLOCK, (blk + 1) * LRU_BLOCK)
        xc = cb_ref[:, cs] + cw_ref[3:4, cs] * u_ref[:, cs]
        for j in range(CONV_W - 1):
            xc = xc + cw_ref[j:j + 1, cs] * sc_ref[j, :, cs]
        a, b = _lru_coeffs(xc, wa_ref, ba_ref[:, cs], wx_ref, bx_ref[:, cs], sp[:, cs], blk)
        h = a * h0_ref[:, cs] + b
        h_ref[:, cs] = h
        y_ref[:, cs] = (h * g_ref[:, cs]).astype(BF16)


def _rnn_sample(u_s, g_s, sc_t, h0, cw, cb, wa, ba, wx, bx, lam):
    nc = D_RNN // TC_R
    vec = lambda rows: pl.BlockSpec((rows, TC_R), lambda c: (0, c))
    gw = pl.BlockSpec((TC_R // LRU_BLOCK, LRU_BLOCK, LRU_BLOCK), lambda c: (c, 0, 0))
    return pl.pallas_call(
        _rnn_sample_kernel,
        grid=(nc,),
        in_specs=[
            vec(DEC_BATCH), vec(DEC_BATCH),
            pl.BlockSpec((CONV_W - 1, DEC_BATCH, TC_R), lambda c: (0, 0, c)),
            vec(DEC_BATCH), vec(CONV_W), vec(1), gw, vec(1), gw, vec(1), vec(1),
        ],
        out_specs=[vec(DEC_BATCH), vec(DEC_BATCH)],
        out_shape=[
            jax.ShapeDtypeStruct((DEC_BATCH, D_RNN), BF16),
            jax.ShapeDtypeStruct((DEC_BATCH, D_RNN), F32),
        ],
        compiler_params=_cparams(("arbitrary",), 32),
        name="rnn_sample",
    )(u_s, g_s, sc_t, h0, cw, cb, wa, ba, wx, bx, lam)


_SCALE = HEAD_DIM ** -0.5
_TRANS_B = (((1,), (1,)), ((), ()))


def _attn_block(i, q_ref, kp_ref, kc_ref, vp_ref, vc_ref, glo_ref, ghi_ref, sink_ref, o_ref):
    rows = GROUP * WINDOW
    r = lax.broadcasted_iota(jnp.int32, (rows, 2 * WINDOW), 0) % WINDOW
    c = lax.broadcasted_iota(jnp.int32, (rows, 2 * WINDOW), 1)
    cmin = jnp.where(i > 0, 0, WINDOW)
    vis = (c > r) & (c <= r + WINDOW) & (c >= cmin)
    rgrp = lax.broadcasted_iota(jnp.int32, (rows, 1), 0) // WINDOW
    for h in range(N_KV):
        ks = slice(h * HEAD_DIM, (h + 1) * HEAD_DIM)
        qh = jnp.concatenate(
            [q_ref[:, (GROUP * h + g) * HEAD_DIM:(GROUP * h + g + 1) * HEAD_DIM] for g in range(GROUP)], axis=0)
        kb = jnp.concatenate([kp_ref[:, ks], kc_ref[:, ks]], axis=0)
        vb = jnp.concatenate([vp_ref[:, ks], vc_ref[:, ks]], axis=0)
        s = lax.dot_general(qh, kb, _TRANS_B, preferred_element_type=F32) * _SCALE
        s = jnp.where(vis, s, NEG_BIG)
        sink = jnp.zeros((rows, 1), F32)
        for g in range(GROUP):
            sink = jnp.where(rgrp == g, sink_ref[GROUP * h + g], sink)
        m = jnp.maximum(jnp.max(s, axis=-1, keepdims=True), sink)
        p = jnp.exp(s - m)
        denom = jnp.sum(p, axis=-1, keepdims=True) + jnp.exp(sink - m)
        o = jnp.dot(p.astype(BF16), vb, preferred_element_type=F32) / denom
        for g in range(GROUP):
            col = (GROUP * h + g) * HEAD_DIM
            gref, gcol = (glo_ref, col) if col < D_ATTN // 2 else (ghi_ref, col - D_ATTN // 2)
            gate = gref[:, gcol:gcol + HEAD_DIM].astype(F32)
            og = o[g * WINDOW:(g + 1) * WINDOW, :] * gate
            o_ref[:, col:col + HEAD_DIM] = og.astype(BF16)


NB_ATT = SEQ // WINDOW
NBLK_ATT = BATCH * NB_ATT


def _attn_side(s, ins, outs, scratch):
    del scratch
    _attn_block(jnp.minimum(s, NBLK_ATT - 1) % NB_ATT, *ins, *outs)


def _attn_side_specs(z, sinks):
    half = D_ATTN // 2
    row = lambda s: jnp.minimum(s, NBLK_ATT - 1)
    prev = lambda s: row(s) - jnp.where(row(s) % NB_ATT > 0, 1, 0)
    in_specs = [
        pl.BlockSpec((WINDOW, D_ATTN), lambda s: (row(s), COL_Q // D_ATTN)),
        pl.BlockSpec((WINDOW, D_KV), lambda s: (prev(s), COL_K // D_KV)),
        pl.BlockSpec((WINDOW, D_KV), lambda s: (row(s), COL_K // D_KV)),
        pl.BlockSpec((WINDOW, D_KV), lambda s: (prev(s), COL_V // D_KV)),
        pl.BlockSpec((WINDOW, D_KV), lambda s: (row(s), COL_V // D_KV)),
        pl.BlockSpec((WINDOW, half), lambda s: (row(s), COL_GATTN // half)),
        pl.BlockSpec((WINDOW, half), lambda s: (row(s), COL_GATTN // half + 1)),
        pl.BlockSpec(memory_space=pltpu.SMEM),
    ]
    out_specs = [pl.BlockSpec((WINDOW, D_ATTN), lambda s: (row(s), 0))]
    out_shape = [jax.ShapeDtypeStruct((M_PROMPT, D_ATTN), BF16)]
    return (z,) * 7 + (sinks,), in_specs, out_specs, out_shape


def _attn_sample_one(q_ref, g_ref, ks_ref, vs_ref, ck_ref, cv_ref, sink_ref, o_ref, kw_ref, vw_ref):
    keep = (WINDOW - 1) * N_KV
    kc = ck_ref[0, N_KV:, :]
    vc = cv_ref[0, N_KV:, :]
    kn = ks_ref[0]
    vn = vs_ref[0]
    kc3 = kc.reshape(WINDOW - 1, N_KV, HEAD_DIM)
    vc3 = vc.reshape(WINDOW - 1, N_KV, HEAD_DIM)
    for g in range(GROUP):
        qg = q_ref[0, g] * _SCALE
        s = jnp.sum(kc3 * qg[None], axis=-1, keepdims=True)
        sn = jnp.sum(kn * qg, axis=-1, keepdims=True)
        sink = sink_ref[g]
        m = jnp.maximum(jnp.maximum(jnp.max(s, axis=0), sn), sink)
        p = jnp.exp(s - m[None])
        pn = jnp.exp(sn - m)
        denom = jnp.sum(p, axis=0) + pn + jnp.exp(sink - m)
        o = (jnp.sum(p * vc3, axis=0) + pn * vn) / denom * g_ref[0, g]
        for h in range(N_KV):
            o_ref[0, pl.ds(GROUP * h + g, 1), :] = o[h:h + 1, :]
    kw_ref[0, 0:keep, :] = kc
    kw_ref[0, keep:, :] = kn
    vw_ref[0, 0:keep, :] = vc
    vw_ref[0, keep:, :] = vn


def _merge_kernel(y_ref, og_ref, wr_ref, wa_ref, mr_ref, ma_ref, o_ref):
    br = jnp.dot(y_ref[...], wr_ref[...], preferred_element_type=F32)
    ba = jnp.dot(og_ref[...], wa_ref[...], preferred_element_type=F32)
    o_ref[...] = (mr_ref[...].astype(F32) * br + ma_ref[...].astype(F32) * ba).astype(BF16)


def _merge_prompt_kernel(y_ref, og_ref, wr_ref, wa_ref, mr_ref, ma_ref,
                         q_ref, g_ref, ks_ref, vs_ref, ck_ref, cv_ref, sink_ref,
                         o_ref, os_ref, kw_ref, vw_ref):
    _attn_sample_one(q_ref, g_ref, ks_ref, vs_ref, ck_ref, cv_ref, sink_ref, os_ref, kw_ref, vw_ref)
    _merge_kernel(y_ref, og_ref, wr_ref, wa_ref, mr_ref, ma_ref, o_ref)


def _merge_specs(tm, rb0):
    act = pl.BlockSpec((tm, D_MODEL), lambda m, n: (m, 0))
    wsp = pl.BlockSpec((D_MODEL, TN_M), lambda m, n: (0, n))
    in_specs = [
        act, act, wsp, wsp,
        pl.BlockSpec((tm, TN_M), lambda m, n: (rb0 + m, n)),
        pl.BlockSpec((tm, TN_M), lambda m, n: (rb0 + m, (COL_MATTN - COL_MRNN) // TN_M + n)),
    ]
    return in_specs, pl.BlockSpec((tm, TN_M), lambda m, n: (m, n))


def _merge_sample(y, og, wr, wa, z):
    in_specs, out_spec = _merge_specs(DEC_BATCH, M_PROMPT // DEC_BATCH)
    return pl.pallas_call(
        _merge_kernel,
        grid=(1, D_MODEL // TN_M),
        in_specs=in_specs,
        out_specs=out_spec,
        out_shape=jax.ShapeDtypeStruct((DEC_BATCH, D_MODEL), BF16),
        compiler_params=_cparams(("arbitrary", "arbitrary"), 32),
        name="merge_sample",
    )(y, og, wr, wa, z, z)


def _merge_prompt(y, og, wr, wa, z, q_s, g_s, k_s, v_s, ck, cv, sinks_g):
    nm, nn = M_PROMPT // TM_M, D_MODEL // TN_M
    assert nm * nn == DEC_BATCH
    in_specs, out_spec = _merge_specs(TM_M, 0)
    qb = pl.BlockSpec((1, GROUP, N_KV, HEAD_DIM), lambda m, n: (m * nn + n, 0, 0, 0))
    nb = pl.BlockSpec((1, N_KV, HEAD_DIM), lambda m, n: (m * nn + n, 0, 0))
    cb = pl.BlockSpec((1, WINDOW * N_KV, HEAD_DIM), lambda m, n: (m * nn + n, 0, 0))
    return pl.pallas_call(
        _merge_prompt_kernel,
        grid=(nm, nn),
        in_specs=in_specs + [qb, qb, nb, nb, cb, cb, pl.BlockSpec((GROUP, N_KV, 1), lambda m, n: (0, 0, 0))],
        out_specs=[out_spec, pl.BlockSpec((1, N_HEADS, HEAD_DIM), lambda m, n: (m * nn + n, 0, 0)), cb, cb],
        out_shape=[
            jax.ShapeDtypeStruct((M_PROMPT, D_MODEL), BF16),
            jax.ShapeDtypeStruct((DEC_BATCH, N_HEADS, HEAD_DIM), F32),
            jax.ShapeDtypeStruct((DEC_BATCH, WINDOW * N_KV, HEAD_DIM), F32),
            jax.ShapeDtypeStruct((DEC_BATCH, WINDOW * N_KV, HEAD_DIM), F32),
        ],
        compiler_params=_cparams(("arbitrary", "arbitrary"), 56),
        name="merge_prompt",
    )(y, og, wr, wa, z, z, q_s, g_s, k_s, v_s, ck, cv, sinks_g)


def _out_kernel(m_ref, wo_ref, x_ref, g_ref, b_ref, y_ref):
    y_ref[...] = DN_ALPHA * x_ref[...] + jnp.dot(m_ref[...], wo_ref[...], preferred_element_type=F32)
    for i in range(y_ref.shape[0] // RG_O):
        rs = slice(i * RG_O, (i + 1) * RG_O)
        r = y_ref[rs, :]
        mu = jnp.mean(r, axis=-1, keepdims=True)
        d = r - mu
        var = jnp.mean(d * d, axis=-1, keepdims=True)
        y_ref[rs, :] = d * lax.rsqrt(var + LN_EPS) * g_ref[...] + b_ref[...]


def _out_proj(merged, wo, x, ln_g, ln_b, tm, name):
    rows = merged.shape[0]
    row = pl.BlockSpec((tm, D_MODEL), lambda i: (i, 0))
    const = lambda i: (0, 0)
    return pl.pallas_call(
        _out_kernel,
        grid=(rows // tm,),
        in_specs=[
            row,
            pl.BlockSpec((D_MODEL, D_MODEL), const, pipeline_mode=pl.Buffered(1)),
            row,
            pl.BlockSpec((1, D_MODEL), const),
            pl.BlockSpec((1, D_MODEL), const),
        ],
        out_specs=row,
        out_shape=jax.ShapeDtypeStruct((rows, D_MODEL), F32),
        compiler_params=_cparams(("arbitrary",), 60),
        name=name,
    )(merged, wo, x, ln_g, ln_b)


def _rope_tables():
    half = HEAD_DIM // 2
    pos = jnp.concatenate([
        jnp.tile(jnp.arange(SEQ, dtype=jnp.int32), BATCH),
        jnp.full((DEC_BATCH,), PAST_LEN, jnp.int32),
    ]).astype(F32)
    inv = ROPE_THETA ** (-jnp.arange(half, dtype=F32) / half)
    ang = pos[:, None] * inv[None, :]
    cos, sin = jnp.cos(ang), jnp.sin(ang)
    return jnp.concatenate([cos, cos], axis=-1), jnp.concatenate([-sin, sin], axis=-1)


def kernel(x_prompt, x_sample, state_conv, state_lru, cache_k_win, cache_v_win, w_in, conv_w, conv_b,
           w_gate_a, b_gate_a, w_gate_x, b_gate_x, lru_lambda, sinks, w_out_rnn, w_out_attn, w_o, ln_g, ln_b):
    xp = x_prompt.reshape(M_PROMPT, D_MODEL)
    xs = x_sample.reshape(DEC_BATCH, D_MODEL)
    x_bf = _cast_rows(xp, xs)
    cos_t, sin_t = _rope_tables()

    (z,) = _inproj(x_bf, w_in[0], cos_t, sin_t, 0, COL_MRNN, "inproj_branches")
    att_args, att_in, att_out, att_shape = _attn_side_specs(z, sinks[0])
    zm, og_p = _inproj(x_bf, w_in[0], cos_t, sin_t, COL_MRNN, N_COLS - COL_MRNN, "inproj_gates",
                       side=_attn_side, side_args=att_args, side_in_specs=att_in,
                       side_out_specs=att_out, side_out_shape=att_shape, vmem_mib=58)

    cw, cb = conv_w[0], conv_b
    wa, wx = w_gate_a[0].astype(BF16), w_gate_x[0].astype(BF16)
    ba, bx, lam = b_gate_a, b_gate_x, lru_lambda

    y_p, conv_p, lru_p, wr_bf, wat_bf, wo_bf = _rnn_prompt(
        z, cw, cb, wa, ba, wx, bx, lam, (w_out_rnn[0], w_out_attn[0], w_o[0]))
    zs = z[M_PROMPT:]
    u_s = zs[:, COL_U:COL_U + D_RNN].astype(F32)
    g_s = zs[:, COL_GRNN:COL_GRNN + D_RNN].astype(F32)
    sc_t = jnp.transpose(state_conv[0], (1, 0, 2))
    y_s, lru_s = _rnn_sample(u_s, g_s, sc_t, state_lru[0], cw, cb, wa, ba, wx, bx, lam)
    conv_s = jnp.stack([state_conv[0, :, 1], state_conv[0, :, 2], u_s], axis=1)

    def by_group(t):
        return t.astype(F32).reshape(DEC_BATCH, N_KV, GROUP, HEAD_DIM).transpose(0, 2, 1, 3)

    q_s = by_group(zs[:, COL_Q:COL_Q + D_ATTN])
    ga_s = by_group(zs[:, COL_GATTN:COL_GATTN + D_ATTN])
    k_s = zs[:, COL_K:COL_K + D_KV].astype(F32).reshape(DEC_BATCH, N_KV, HEAD_DIM)
    v_s = zs[:, COL_V:COL_V + D_KV].astype(F32).reshape(DEC_BATCH, N_KV, HEAD_DIM)
    ck = cache_k_win[0].reshape(DEC_BATCH, WINDOW * N_KV, HEAD_DIM)
    cv = cache_v_win[0].reshape(DEC_BATCH, WINDOW * N_KV, HEAD_DIM)
    sinks_g = sinks[0].reshape(N_KV, GROUP).T.reshape(GROUP, N_KV, 1)

    merged_p, og_s, kw_s, vw_s = _merge_prompt(
        y_p, og_p, wr_bf, wat_bf, zm, q_s, ga_s, k_s, v_s, ck, cv, sinks_g)
    og_s = og_s.reshape(DEC_BATCH, D_ATTN).astype(BF16)
    merged_s = _merge_sample(y_s, og_s, wr_bf, wat_bf, zm)
    out_p = _out_proj(merged_p, wo_bf, xp, ln_g, ln_b, TM_O, "out_proj_prompt")
    out_s = _out_proj(merged_s, wo_bf, xs, ln_g, ln_b, DEC_BATCH, "out_proj_sample")

    kv_tail = jnp.stack([z[(b + 1) * SEQ - WINDOW:(b + 1) * SEQ, COL_K:COL_K + 2 * D_KV] for b in range(BATCH)])
    kv_tail = kv_tail.astype(F32)
    kw_p = kv_tail[:, :, :D_KV].reshape(BATCH, WINDOW, N_KV, HEAD_DIM)
    vw_p = kv_tail[:, :, D_KV:].reshape(BATCH, WINDOW, N_KV, HEAD_DIM)

    return (
        out_p.reshape(BATCH, SEQ, D_MODEL),
        out_s.reshape(DEC_BATCH, 1, D_MODEL),
        conv_p[None],
        lru_p.reshape(1, BATCH, D_RNN),
        kw_p[None],
        vw_p[None],
        conv_s[None],
        lru_s[None],
        kw_s.reshape(1, DEC_BATCH, WINDOW, N_KV, HEAD_DIM),
        vw_s.reshape(1, DEC_BATCH, WINDOW, N_KV, HEAD_DIM),
    )
```
